```python
import math
import jax, jax.numpy as jnp
from jax import lax
import numpy as np

D_MODEL = 2048
BATCH = 4
SEQ = 4096
DEPTH = 4

HEAD_DIM = 128
N_HEADS_A = 12
N_HEADS_B = 12
N_HEADS_MEM = 4
N_MEM = 256
D_A = N_HEADS_A * HEAD_DIM
D_MEMQ = N_HEADS_MEM * HEAD_DIM
Q_LORA = 512
KV_LORA = 512
NOPE_DIM = 128
ROPE_DIM = 64
QK_DIM_B = NOPE_DIM + ROPE_DIM
V_DIM_B = 128
D_MIX_A = D_A + D_MEMQ
D_MIX_B = N_HEADS_B * V_DIM_B + D_MEMQ
ROPE_THETA = 10000.0
A_IN = 4 * D_A + N_HEADS_A + D_MEMQ
B_IN = Q_LORA + D_MEMQ
D_FF = 5632
N_EXPERTS = 8
TOP_K = 2
D_FF_EXPERT = 7168
BLOCK_Q = 128
N_A_LAYERS = DEPTH // 2
N_B_LAYERS = DEPTH - N_A_LAYERS
N_DENSE = (DEPTH + 1) // 2
N_MOE = DEPTH // 2
EPS = 1e-6
FORGET_BIAS_MEAN = 2.0
POS_OFFSET_MAX = 1024

kernel_name = "fox_yoco_mla_memory_moe_trunk"


def rms_norm(x, gain):
    xf = x.astype(jnp.float32)
    y = xf * lax.rsqrt(jnp.mean(xf * xf, axis=-1, keepdims=True) + EPS)
    return (y * gain.astype(jnp.float32)).astype(x.dtype)


def rope_tail(x, positions):
    half = ROPE_DIM // 2
    inv_freq = jnp.exp(-math.log(ROPE_THETA) * jnp.arange(half, dtype=jnp.float32) / half)
    ang = positions.astype(jnp.float32)[:, :, None] * inv_freq
    cos = jnp.cos(ang)[:, :, None, :]
    sin = jnp.sin(ang)[:, :, None, :]
    xr = x[..., -ROPE_DIM:].astype(jnp.float32)
    x1, x2 = xr[..., :half], xr[..., half:]
    rot = jnp.concatenate([x1 * cos - x2 * sin, x2 * cos + x1 * sin], axis=-1).astype(x.dtype)
    return jnp.concatenate([x[..., :-ROPE_DIM], rot], axis=-1)


def causal_block_attention(q, k, v, log_forget_cum=None):
    S = q.shape[1]
    outs = []
    for i in range(S // BLOCK_Q):
        q0, q1 = i * BLOCK_Q, (i + 1) * BLOCK_Q
        s = jnp.einsum('bqhd,bkhd->bhqk', q[:, q0:q1], k[:, :q1]).astype(jnp.float32)
        if log_forget_cum is not None:
            s = s + log_forget_cum[:, :, q0:q1, None] - log_forget_cum[:, :, None, :q1]
        qpos = q0 + jnp.arange(BLOCK_Q)
        causal = jnp.arange(q1)[None, :] <= qpos[:, None]
        s = jnp.where(causal, s, -jnp.inf)
        p = jax.nn.softmax(s, axis=-1).astype(v.dtype)
        outs.append(jnp.einsum('bhqk,bkhd->bqhd', p, v[:, :q1]))
    return jnp.concatenate(outs, axis=1)


def fox_mixer(hn, w_in, b_forget, q_gain, k_gain):
    B, S, _ = hn.shape
    proj = hn @ w_in
    q, k, v, g, f_logit, mq = jnp.split(
        proj, [D_A, 2 * D_A, 3 * D_A, 4 * D_A, 4 * D_A + N_HEADS_A], axis=-1)
    heads = lambda t: t.reshape(B, S, N_HEADS_A, HEAD_DIM)
    q = rms_norm(heads(q), q_gain) * (HEAD_DIM ** -0.5)
    k = rms_norm(heads(k), k_gain)
    log_f = jax.nn.log_sigmoid(f_logit.astype(jnp.float32) + b_forget.astype(jnp.float32))
    F = jnp.cumsum(log_f, axis=1).transpose(0, 2, 1)
    o = causal_block_attention(q, k, heads(v), F)
    o = o * jax.nn.sigmoid(heads(g))
    return o.reshape(B, S, D_A), mq


def shared_mla_kv(h, positions, s_norm, s_w_dkv, s_kv_latent_norm, s_w_ukv, s_k_gain):
    B, S, _ = h.shape
    ckr = rms_norm(h, s_norm) @ s_w_dkv
    c_kv = rms_norm(ckr[..., :KV_LORA], s_kv_latent_norm)
    k_rope = ckr[..., KV_LORA:]
    kv = (c_kv @ s_w_ukv).reshape(B, S, N_HEADS_B, NOPE_DIM + V_DIM_B)
    k_nope, v = kv[..., :NOPE_DIM], kv[..., NOPE_DIM:]
    k = jnp.concatenate(
        [k_nope, jnp.broadcast_to(k_rope[:, :, None, :], (B, S, N_HEADS_B, ROPE_DIM))], axis=-1)
    k = rope_tail(rms_norm(k, s_k_gain), positions)
    return k, v


def mla_mixer(hn, positions, k, v, w_in, q_latent_norm, w_uq, q_gain):
    B, S, _ = hn.shape
    cq, mq = jnp.split(hn @ w_in, [Q_LORA], axis=-1)
    q = (rms_norm(cq, q_latent_norm) @ w_uq).reshape(B, S, N_HEADS_B, QK_DIM_B)
    q = rope_tail(rms_norm(q, q_gain), positions) * (QK_DIM_B ** -0.5)
    o = causal_block_attention(q, k, v)
    return o.reshape(B, S, N_HEADS_B * V_DIM_B), mq


def memory_mixer(mq, mem, mem_norm, w_kv, q_gain, k_gain):
    B, S, _ = mq.shape
    M = mem.shape[1]
    q = rms_norm(mq.reshape(B, S, N_HEADS_MEM, HEAD_DIM), q_gain) * (HEAD_DIM ** -0.5)
    kv = (rms_norm(mem, mem_norm) @ w_kv).reshape(B, M, 2 * N_HEADS_MEM, HEAD_DIM)
    mk, mv = jnp.split(kv, 2, axis=2)
    mk = rms_norm(mk, k_gain)
    s = jnp.einsum('bqhd,bmhd->bhqm', q, mk).astype(jnp.float32)
    p = jax.nn.softmax(s, axis=-1).astype(mv.dtype)
    return jnp.einsum('bhqm,bmhd->bqhd', p, mv).reshape(B, S, D_MEMQ)


def swiglu(h, w_gate, w_up, w_down):
    return (jax.nn.silu(h @ w_gate) * (h @ w_up)) @ w_down


def moe_swiglu(h, w_router, w_gate, w_up, w_down):
    logits = (h @ w_router).astype(jnp.float32)
    top_vals, top_idx = lax.top_k(logits, TOP_K)
    top_w = jax.nn.softmax(top_vals, axis=-1)
    combine = jnp.sum(jax.nn.one_hot(top_idx, N_EXPERTS, dtype=jnp.float32) * top_w[..., None], axis=-2)
    combine = combine.astype(h.dtype)
    out = jnp.zeros_like(h)
    for e in range(N_EXPERTS):
        out = out + combine[..., e:e + 1] * swiglu(h, w_gate[e], w_up[e], w_down[e])
    return out


def setup_inputs(seed: int = 0) -> dict:
    key = jax.random.key(seed)
    ks = iter(jax.random.split(key, 40))
    f32 = jnp.float32

    def w(shape, fan_in):
        return jax.random.normal(next(ks), shape, f32) * (fan_in ** -0.5)

    def gain(shape):
        return 1.0 + 0.02 * jax.random.normal(next(ks), shape, f32)

    x = jax.random.normal(next(ks), (BATCH, SEQ, D_MODEL), f32)
    mem = jax.random.normal(next(ks), (BATCH, N_MEM, D_MODEL), f32)
    positions = (jnp.arange(SEQ, dtype=jnp.int32)[None, :]
                 + jax.random.randint(next(ks), (BATCH, 1), 0, POS_OFFSET_MAX, dtype=jnp.int32))
    return {
        'x': x, 'mem': mem, 'positions': positions,
        'a_norm': gain((N_A_LAYERS, D_MODEL)),
        'a_w_in': w((N_A_LAYERS, D_MODEL, A_IN), D_MODEL),
        'a_b_forget': FORGET_BIAS_MEAN + 0.5 * jax.random.normal(next(ks), (N_A_LAYERS, N_HEADS_A), f32),
        'a_q_gain': gain((N_A_LAYERS, HEAD_DIM)),
        'a_k_gain': gain((N_A_LAYERS, HEAD_DIM)),
        'a_w_out': w((N_A_LAYERS, D_MIX_A, D_MODEL), D_MIX_A),
        'b_norm': gain((N_B_LAYERS, D_MODEL)),
        'b_w_in': w((N_B_LAYERS, D_MODEL, B_IN), D_MODEL),
        'b_q_latent_norm': gain((N_B_LAYERS, Q_LORA)),
        'b_w_uq': w((N_B_LAYERS, Q_LORA, N_HEADS_B * QK_DIM_B), Q_LORA),
        'b_q_gain': gain((N_B_LAYERS, QK_DIM_B)),
        'b_w_out': w((N_B_LAYERS, D_MIX_B, D_MODEL), D_MIX_B),
        's_norm': gain((D_MODEL,)),
        's_w_dkv': w((D_MODEL, KV_LORA + ROPE_DIM), D_MODEL),
        's_kv_latent_norm': gain((KV_LORA,)),
        's_w_ukv': w((KV_LORA, N_HEADS_B * (NOPE_DIM + V_DIM_B)), KV_LORA),
        's_k_gain': gain((QK_DIM_B,)),
        'm_norm': gain((DEPTH, D_MODEL)),
        'm_w_kv': w((DEPTH, D_MODEL, 2 * D_MEMQ), D_MODEL),
        'm_q_gain': gain((DEPTH, HEAD_DIM)),
        'm_k_gain': gain((DEPTH, HEAD_DIM)),
        'f_norm': gain((DEPTH, D_MODEL)),
        'd_w_gate': w((N_DENSE, D_MODEL, D_FF), D_MODEL),
        'd_w_up': w((N_DENSE, D_MODEL, D_FF), D_MODEL),
        'd_w_down': w((N_DENSE, D_FF, D_MODEL), D_FF),
        'e_router': w((N_MOE, D_MODEL, N_EXPERTS), D_MODEL),
        'e_w_gate': w((N_MOE, N_EXPERTS, D_MODEL, D_FF_EXPERT), D_MODEL),
        'e_w_up': w((N_MOE, N_EXPERTS, D_MODEL, D_FF_EXPERT), D_MODEL),
        'e_w_down': w((N_MOE, N_EXPERTS, D_FF_EXPERT, D_MODEL), D_FF_EXPERT),
    }


def reference(x, mem, positions, a_norm, a_w_in, a_b_forget, a_q_gain, a_k_gain, a_w_out,
              b_norm, b_w_in, b_q_latent_norm, b_w_uq, b_q_gain, b_w_out,
              s_norm, s_w_dkv, s_kv_latent_norm, s_w_ukv, s_k_gain,
              m_norm, m_w_kv, m_q_gain, m_k_gain,
              f_norm, d_w_gate, d_w_up, d_w_down,
              e_router, e_w_gate, e_w_up, e_w_down):
    h = x
    shared_k = shared_v = None
    for l in range(DEPTH):
        if l == N_A_LAYERS:
            shared_k, shared_v = shared_mla_kv(h, positions, s_norm, s_w_dkv,
                                               s_kv_latent_norm, s_w_ukv, s_k_gain)
        if l < N_A_LAYERS:
            hn = rms_norm(h, a_norm[l])
            mix, mq = fox_mixer(hn, a_w_in[l], a_b_forget[l], a_q_gain[l], a_k_gain[l])
            w_out = a_w_out[l]
        else:
            j = l - N_A_LAYERS
            hn = rms_norm(h, b_norm[j])
            mix, mq = mla_mixer(hn, positions, shared_k, shared_v, b_w_in[j],
                                b_q_latent_norm[j], b_w_uq[j], b_q_gain[j])
            w_out = b_w_out[j]
        mo = memory_mixer(mq, mem, m_norm[l], m_w_kv[l], m_q_gain[l], m_k_gain[l])
        h = h + jnp.concatenate([mix, mo], axis=-1) @ w_out
        hn = rms_norm(h, f_norm[l])
        if l % 2 == 0:
            i = l // 2
            h = h + swiglu(hn, d_w_gate[i], d_w_up[i], d_w_down[i])
        else:
            i = l // 2
            h = h + moe_swiglu(hn, e_router[i], e_w_gate[i], e_w_up[i], e_w_down[i])
    return h
```

```python
import functools
import math

import jax
import jax.numpy as jnp
from jax import lax
from jax.experimental import pallas as pl
from jax.experimental.pallas import tpu as pltpu

F32 = jnp.float32
BF16 = jnp.bfloat16

EPS = 1e-6
HEAD_DIM = 128
N_HEADS = 12
N_HEADS_MEM = 4
NOPE_DIM = 128
ROPE_DIM = 64
ROPE_HALF = ROPE_DIM // 2
QK_DIM_B = NOPE_DIM + ROPE_DIM
MLA_PAD = 256
Q_LORA = 512
KV_LORA = 512
ROPE_THETA = 10000.0
N_EXPERTS = 8
LANES = 128
VMEM_LIMIT = 56 * 1024 * 1024


def _params(n_grid, vmem=VMEM_LIMIT):
    return pltpu.CompilerParams(dimension_semantics=("arbitrary",) * n_grid, vmem_limit_bytes=vmem)


def _rms(x, gain):
    ms = jnp.mean(x * x, axis=-1, keepdims=True)
    return x * lax.rsqrt(ms + EPS) * gain


def _norm_proj_kernel(*refs, modes, tn, with_f):
    if with_f:
        h_ref, ng_ref, w_ref, cg_ref, wf_ref, o_ref, f_ref, hn_ref = refs
    else:
        h_ref, ng_ref, w_ref, cg_ref, o_ref, hn_ref = refs
    j = pl.program_id(1)

    @pl.when(j == 0)
    def _():
        hn = _rms(h_ref[...], ng_ref[...]).astype(BF16)
        hn_ref[...] = hn
        if with_f:
            f_ref[...] = jnp.dot(hn, wf_ref[...], preferred_element_type=F32)

    acc = jnp.dot(hn_ref[...], w_ref[...], preferred_element_type=F32)
    for lo, hi, gs in modes:
        @pl.when((j >= lo) & (j < hi))
        def _(gs=gs):
            if gs == 0:
                o_ref[...] = acc.astype(o_ref.dtype)
            else:
                for g in range(tn // gs):
                    sl = slice(g * gs, (g + 1) * gs)
                    o_ref[:, sl] = _rms(acc[:, sl], cg_ref[:, sl]).astype(o_ref.dtype)


def norm_proj(h, ngain, w, colgain, modes, *, tm, tn, wf=None):
    T, D = h.shape
    N = w.shape[1]
    with_f = wf is not None
    in_specs = [
        pl.BlockSpec((tm, D), lambda i, j: (i, 0)),
        pl.BlockSpec((1, D), lambda i, j: (0, 0)),
        pl.BlockSpec((D, tn), lambda i, j: (0, j)),
        pl.BlockSpec((1, tn), lambda i, j: (0, j)),
    ]
    args = [h, ngain.reshape(1, D), w, colgain.reshape(1, N)]
    out_shape = [jax.ShapeDtypeStruct((T, N), BF16)]
    out_specs = [pl.BlockSpec((tm, tn), lambda i, j: (i, j))]
    if with_f:
        in_specs.append(pl.BlockSpec((D, LANES), lambda i, j: (0, 0)))
        args.append(wf)
        out_shape.append(jax.ShapeDtypeStruct((T, LANES), F32))
        out_specs.append(pl.BlockSpec((tm, LANES), lambda i, j: (i, 0)))
    outs = pl.pallas_call(
        functools.partial(_norm_proj_kernel, modes=modes, tn=tn, with_f=with_f),
        grid=(T // tm, N // tn),
        in_specs=in_specs, out_specs=out_specs, out_shape=out_shape,
        scratch_shapes=[pltpu.VMEM((tm, D), BF16)],
        compiler_params=_params(2), name="norm_proj",
    )(*args)
    return outs if with_f else outs[0]


def _forget_scan_kernel(x_ref, b_ref, tri_ref, o_ref, *, n_chunks):
    z = x_ref[...] + b_ref[...]
    log_f = jnp.minimum(z, 0.0) - jnp.log1p(jnp.exp(-jnp.abs(z)))
    tri = tri_ref[...]
    carry = jnp.zeros((z.shape[0], 1), F32)
    for c in range(n_chunks):
        sl = slice(c * LANES, (c + 1) * LANES)
        cs = jnp.dot(log_f[:, sl], tri, precision=lax.Precision.HIGHEST,
                     preferred_element_type=F32) + carry
        o_ref[:, sl] = cs
        carry = cs[:, LANES - 1:LANES]


def forget_scan(f_logit_t, bias_col):
    R, S = f_logit_t.shape
    tri = (lax.broadcasted_iota(jnp.int32, (LANES, LANES), 0)
           <= lax.broadcasted_iota(jnp.int32, (LANES, LANES), 1)).astype(F32)
    return pl.pallas_call(
        functools.partial(_forget_scan_kernel, n_chunks=S // LANES),
        out_shape=jax.ShapeDtypeStruct((R, S), F32), name="forget_scan",
    )(f_logit_t, bias_col, tri)


def _flash_kernel(*refs, tq, tk, has_f, has_g):
    it = iter(refs)
    q_ref, k_ref, v_ref = next(it), next(it), next(it)
    f_ref = next(it) if has_f else None
    g_ref = next(it) if has_g else None
    o_ref, m_ref, l_ref, acc_ref = next(it), next(it), next(it), next(it)
    i = pl.program_id(2)
    m_ref[...] = jnp.full(m_ref.shape, -jnp.inf, F32)
    l_ref[...] = jnp.zeros(l_ref.shape, F32)
    acc_ref[...] = jnp.zeros(acc_ref.shape, F32)
    q = q_ref[...]

    def step(j, masked):
        off = pl.multiple_of(j * tk, tk)
        s = lax.dot_general(q, k_ref[pl.ds(off, tk), :], (((1,), (1,)), ((), ())),
                            preferred_element_type=F32)
        if has_f:
            s = s - f_ref[0, :, pl.ds(off, tk)]
        if masked:
            row = i * tq + lax.broadcasted_iota(jnp.int32, (tq, tk), 0)
            col = off + lax.broadcasted_iota(jnp.int32, (tq, tk), 1)
            s = jnp.where(col <= row, s, -jnp.inf)
        m_prev = m_ref[...]
        m_new = jnp.maximum(m_prev, jnp.max(s, axis=-1, keepdims=True))
        alpha = jnp.exp(m_prev - m_new)
        p = jnp.exp(s - m_new)
        l_ref[...] = alpha * l_ref[...] + jnp.sum(p, axis=-1, keepdims=True)
        acc_ref[...] = alpha * acc_ref[...] + jnp.dot(
            p.astype(BF16), v_ref[pl.ds(off, tk), :], preferred_element_type=F32)
        m_ref[...] = m_new

    r = tq // tk

    def full_step(j, c):
        step(j, False)
        return c

    lax.fori_loop(0, i * r, full_step, 0)
    for d in range(r):
        step(i * r + d, True)
    o = acc_ref[...] / l_ref[...]
    if has_g:
        o = o * jax.nn.sigmoid(g_ref[...].astype(F32))
    o_ref[...] = o.astype(o_ref.dtype)


def causal_attention(q_arr, q_col0, k_arr, k_col0, v_arr, v_col0, *, B, S, dk, dv, tq, tk,
                     f_cum=None, g_arr=None, g_col0=0):
    nq = S // tq
    T = B * S
    has_f, has_g = f_cum is not None, g_arr is not None
    in_specs = [
        pl.BlockSpec((tq, dk), lambda b, h, i: (b * nq + i, q_col0 + h)),
        pl.BlockSpec((S, dk), lambda b, h, i: (b, k_col0 + h)),
        pl.BlockSpec((S, dv), lambda b, h, i: (b, v_col0 + h)),
    ]
    args = [q_arr, k_arr, v_arr]
    if has_f:
        in_specs.append(pl.BlockSpec((1, 1, S), lambda b, h, i: (b * N_HEADS + h, 0, 0)))
        args.append(f_cum)
    if has_g:
        in_specs.append(pl.BlockSpec((tq, dv), lambda b, h, i: (b * nq + i, g_col0 + h)))
        args.append(g_arr)
    return pl.pallas_call(
        functools.partial(_flash_kernel, tq=tq, tk=tk, has_f=has_f, has_g=has_g),
        grid=(B, N_HEADS, nq),
        in_specs=in_specs,
        out_specs=pl.BlockSpec((tq, dv), lambda b, h, i: (b * nq + i, h)),
        out_shape=jax.ShapeDtypeStruct((T, N_HEADS * dv), BF16),
        scratch_shapes=[pltpu.VMEM((tq, 1), F32), pltpu.VMEM((tq, 1), F32),
                        pltpu.VMEM((tq, dv), F32)],
        compiler_params=_params(3), name="causal_attention",
    )(*args)


def _mem_kv_kernel(mem_ref, ng_ref, w_ref, kg_ref, o_ref, *, n_norm_cols):
    x = _rms(mem_ref[...], ng_ref[...]).astype(BF16)
    acc = jnp.dot(x, w_ref[...], preferred_element_type=F32)
    for g in range(acc.shape[1] // HEAD_DIM):
        sl = slice(g * HEAD_DIM, (g + 1) * HEAD_DIM)
        if g * HEAD_DIM < n_norm_cols:
            o_ref[:, sl] = _rms(acc[:, sl], kg_ref[...]).astype(o_ref.dtype)
        else:
            o_ref[:, sl] = acc[:, sl].astype(o_ref.dtype)


def mem_kv(mem2d, m_norm, m_w_kv, m_k_gain):
    L, D, N = m_w_kv.shape
    R = mem2d.shape[0]
    return pl.pallas_call(
        functools.partial(_mem_kv_kernel, n_norm_cols=N // 2),
        grid=(L,),
        in_specs=[
            pl.BlockSpec((R, D), lambda l: (0, 0)),
            pl.BlockSpec((None, 1, D), lambda l: (l, 0, 0)),
            pl.BlockSpec((None, D, N), lambda l: (l, 0, 0)),
            pl.BlockSpec((None, 1, HEAD_DIM), lambda l: (l, 0, 0)),
        ],
        out_specs=pl.BlockSpec((None, R, N), lambda l: (l, 0, 0)),
        out_shape=jax.ShapeDtypeStruct((L, R, N), BF16),
        compiler_params=_params(1), name="mem_kv",
    )(mem2d, m_norm.reshape(L, 1, D), m_w_kv, m_k_gain.reshape(L, 1, HEAD_DIM))


def _mem_attn_kernel(q_ref, k_ref, v_ref, o_ref):
    s = lax.dot_general(q_ref[...], k_ref[...], (((1,), (1,)), ((), ())),
                        preferred_element_type=F32)
    m = jnp.max(s, axis=-1, keepdims=True)
    p = jnp.exp(s - m)
    l = jnp.sum(p, axis=-1, keepdims=True)
    o = jnp.dot(p.astype(BF16), v_ref[...], preferred_element_type=F32) / l
    o_ref[...] = o.astype(o_ref.dtype)


def mem_attention(q_arr, q_col0, mkv, layer, *, B, S, M, tq):
    nq = S // tq
    T = B * S
    return pl.pallas_call(
        _mem_attn_kernel,
        grid=(B, N_HEADS_MEM, nq),
        in_specs=[
            pl.BlockSpec((tq, HEAD_DIM), lambda b, h, i: (b * nq + i, q_col0 + h)),
            pl.BlockSpec((None, M, HEAD_DIM), lambda b, h, i: (layer, b, h)),
            pl.BlockSpec((None, M, HEAD_DIM), lambda b, h, i: (layer, b, N_HEADS_MEM + h)),
        ],
        out_specs=pl.BlockSpec((tq, HEAD_DIM), lambda b, h, i: (b * nq + i, h)),
        out_shape=jax.ShapeDtypeStruct((T, N_HEADS_MEM * HEAD_DIM), BF16),
        compiler_params=_params(3), name="mem_attention",
    )(q_arr, mkv, mkv)


def _out_proj_kernel(*refs, with_router):
    if with_router:
        xa_ref, xm_ref, wa_ref, wm_ref, h_ref, fg_ref, wr_ref, o_ref, ids_ref, wts_ref = refs
    else:
        xa_ref, xm_ref, wa_ref, wm_ref, h_ref, o_ref = refs
    acc = jnp.dot(xa_ref[...], wa_ref[...], preferred_element_type=F32)
    acc = acc + jnp.dot(xm_ref[...], wm_ref[...], preferred_element_type=F32)
    h_new = h_ref[...] + acc
    o_ref[...] = h_new
    if with_router:
        hn = _rms(h_new, fg_ref[...])
        logits = jnp.dot(hn, wr_ref[...], precision=lax.Precision.HIGHEST,
                         preferred_element_type=F32)
        lane = lax.broadcasted_iota(jnp.int32, logits.shape, 1)
        lane_f = lane.astype(F32)
        logits = jnp.where(lane < N_EXPERTS, logits, -jnp.inf)
        l1 = jnp.max(logits, axis=-1, keepdims=True)
        i1 = jnp.min(jnp.where(logits == l1, lane_f, float(LANES)), axis=-1, keepdims=True)
        rest = jnp.where(lane_f == i1, -jnp.inf, logits)
        l2 = jnp.max(rest, axis=-1, keepdims=True)
        i2 = jnp.min(jnp.where(rest == l2, lane_f, float(LANES)), axis=-1, keepdims=True)
        e = jnp.exp(l2 - l1)
        w1 = 1.0 / (1.0 + e)
        w2 = e / (1.0 + e)
        ids_ref[...] = jnp.where(lane == 0, i1, jnp.where(lane == 1, i2, 0.0))
        wts_ref[...] = jnp.where(lane == 0, w1, jnp.where(lane == 1, w2, 0.0))


def out_proj(xa, xm, wa, wm, h, *, tm, f_gain=None, w_router=None):
    T, D = h.shape
    Ka, Km = xa.shape[1], xm.shape[1]
    with_router = w_router is not None
    in_specs = [
        pl.BlockSpec((tm, Ka), lambda i: (i, 0)),
        pl.BlockSpec((tm, Km), lambda i: (i, 0)),
        pl.BlockSpec((Ka, D), lambda i: (0, 0)),
        pl.BlockSpec((Km, D), lambda i: (0, 0)),
        pl.BlockSpec((tm, D), lambda i: (i, 0)),
    ]
    args = [xa, xm, wa, wm, h]
    out_shape = [jax.ShapeDtypeStruct((T, D), F32)]
    out_specs = [pl.BlockSpec((tm, D), lambda i: (i, 0))]
    if with_router:
        in_specs += [pl.BlockSpec((1, D), lambda i: (0, 0)),
                     pl.BlockSpec((D, LANES), lambda i: (0, 0))]
        args += [f_gain.reshape(1, D), w_router]
        out_shape += [jax.ShapeDtypeStruct((T, LANES), F32)] * 2
        out_specs += [pl.BlockSpec((tm, LANES), lambda i: (i, 0))] * 2
    outs = pl.pallas_call(
        functools.partial(_out_proj_kernel, with_router=with_router),
        grid=(T // tm,),
        in_specs=in_specs, out_specs=out_specs, out_shape=out_shape,
        compiler_params=_params(1), name="out_proj",
    )(*args)
    return outs if with_router else outs[0]


def _swiglu_step(hn, wg_ref, wu_ref, wd_ref):
    g = jnp.dot(hn, wg_ref[...], preferred_element_type=F32)
    u = jnp.dot(hn, wu_ref[...], preferred_element_type=F32)
    a = (g * jax.nn.sigmoid(g)) * u
    return jnp.dot(a.astype(BF16), wd_ref[...], preferred_element_type=F32)


def _dense_ffn_kernel(h_ref, fg_ref, wg_ref, wu_ref, wd_ref, o_ref, hn_ref, acc_ref):
    f = pl.program_id(1)

    @pl.when(f == 0)
    def _():
        hn_ref[...] = _rms(h_ref[...], fg_ref[...]).astype(BF16)
        acc_ref[...] = jnp.zeros(acc_ref.shape, F32)

    acc_ref[...] += _swiglu_step(hn_ref[...], wg_ref, wu_ref, wd_ref)

    @pl.when(f == pl.num_programs(1) - 1)
    def _():
        o_ref[...] = h_ref[...] + acc_ref[...]


def dense_ffn(h, f_gain, wg, wu, wd, *, tm, tf):
    T, D = h.shape
    FF = wg.shape[1]
    return pl.pallas_call(
        _dense_ffn_kernel,
        grid=(T // tm, FF // tf),
        in_specs=[
            pl.BlockSpec((tm, D), lambda i, f: (i, 0)),
            pl.BlockSpec((1, D), lambda i, f: (0, 0)),
            pl.BlockSpec((D, tf), lambda i, f: (0, f)),
            pl.BlockSpec((D, tf), lambda i, f: (0, f)),
            pl.BlockSpec((tf, D), lambda i, f: (f, 0)),
        ],
        out_specs=pl.BlockSpec((tm, D), lambda i, f: (i, 0)),
        out_shape=jax.ShapeDtypeStruct((T, D), F32),
        scratch_shapes=[pltpu.VMEM((tm, D), BF16), pltpu.VMEM((tm, D), F32)],
        compiler_params=_params(2), name="dense_ffn",
    )(h, f_gain.reshape(1, D), wg, wu, wd)


def _moe_ffn_kernel(te_ref, nv_ref, x_ref, fg_ref, wg_ref, wu_ref, wd_ref, y_ref, hn_ref, acc_ref):
    i, f = pl.program_id(0), pl.program_id(1)

    @pl.when(i < nv_ref[0])
    def _():
        @pl.when(f == 0)
        def _():
            hn_ref[...] = _rms(x_ref[...], fg_ref[...]).astype(BF16)
            acc_ref[...] = jnp.zeros(acc_ref.shape, F32)

        acc_ref[...] += _swiglu_step(hn_ref[...], wg_ref, wu_ref, wd_ref)

        @pl.when(f == pl.num_programs(1) - 1)
        def _():
            y_ref[...] = acc_ref[...]

    @pl.when((i >= nv_ref[0]) & (f == 0))
    def _():
        y_ref[...] = jnp.zeros(y_ref.shape, y_ref.dtype)


def moe_ffn(x_sorted, tile_expert, n_valid, f_gain, wg, wu, wd, *, tm, tf):
    P, D = x_sorted.shape
    FF = wg.shape[2]
    nf = FF // tf

    def row_map(i, f, te, nv):
        return (jnp.minimum(i, nv[0] - 1), 0)

    def f_idx(i, f, nv):
        return jnp.where(i < nv[0], f, nf - 1)

    grid_spec = pltpu.PrefetchScalarGridSpec(
        num_scalar_prefetch=2,
        grid=(P // tm, nf),
        in_specs=[
            pl.BlockSpec((tm, D), row_map),
            pl.BlockSpec((1, D), lambda i, f, te, nv: (0, 0)),
            pl.BlockSpec((None, D, tf), lambda i, f, te, nv: (te[i], 0, f_idx(i, f, nv))),
            pl.BlockSpec((None, D, tf), lambda i, f, te, nv: (te[i], 0, f_idx(i, f, nv))),
            pl.BlockSpec((None, tf, D), lambda i, f, te, nv: (te[i], f_idx(i, f, nv), 0)),
        ],
        out_specs=pl.BlockSpec((tm, D), lambda i, f, te, nv: (i, 0)),
        scratch_shapes=[pltpu.VMEM((tm, D), BF16), pltpu.VMEM((tm, D), F32)],
    )
    return pl.pallas_call(
        _moe_ffn_kernel, grid_spec=grid_spec,
        out_shape=jax.ShapeDtypeStruct((P, D), F32),
        compiler_params=_params(2), name="moe_ffn",
    )(tile_expert, n_valid, x_sorted, f_gain.reshape(1, D), wg, wu, wd)


def _gather_kernel(src_ref, nv_ref, h_any, o_ref, sem, *, tm):
    i = pl.program_id(0)

    @pl.when(i < nv_ref[0])
    def _():
        base = i * tm

        def issue(r, c):
            tok = src_ref[base + r]
            pltpu.make_async_copy(h_any.at[pl.ds(tok, 1), :], o_ref.at[pl.ds(r, 1), :], sem).start()
            return c

        lax.fori_loop(0, tm, issue, 0)

        def drain(r, c):
            pltpu.make_async_copy(h_any.at[pl.ds(0, 1), :], o_ref.at[pl.ds(r, 1), :], sem).wait()
            return c

        lax.fori_loop(0, tm, drain, 0)

    @pl.when(i >= nv_ref[0])
    def _():
        o_ref[...] = jnp.zeros(o_ref.shape, o_ref.dtype)


def gather_rows(h, src_tok, n_valid, *, tm):
    T, D = h.shape
    P = src_tok.shape[0]
    grid_spec = pltpu.PrefetchScalarGridSpec(
        num_scalar_prefetch=2,
        grid=(P // tm,),
        in_specs=[pl.BlockSpec(memory_space=pl.ANY)],
        out_specs=pl.BlockSpec((tm, D), lambda i, src, nv: (i, 0)),
        scratch_shapes=[pltpu.SemaphoreType.DMA(())],
    )
    return pl.pallas_call(
        functools.partial(_gather_kernel, tm=tm), grid_spec=grid_spec,
        out_shape=jax.ShapeDtypeStruct((P, D), h.dtype),
        compiler_params=_params(1), name="gather_rows",
    )(src_tok, n_valid, h)


def _combine_kernel(p1_ref, p2_ref, h_ref, w_ref, y_any, o_ref, buf, sem, *, tc):
    base = pl.program_id(0) * tc

    def issue(r, c):
        pltpu.make_async_copy(y_any.at[pl.ds(p1_ref[base + r], 1), :],
                              buf.at[0, pl.ds(r, 1), :], sem).start()
        pltpu.make_async_copy(y_any.at[pl.ds(p2_ref[base + r], 1), :],
                              buf.at[1, pl.ds(r, 1), :], sem).start()
        return c

    lax.fori_loop(0, tc, issue, 0)

    def drain(r, c):
        pltpu.make_async_copy(y_any.at[pl.ds(0, 1), :], buf.at[0, pl.ds(r, 1), :], sem).wait()
        pltpu.make_async_copy(y_any.at[pl.ds(0, 1), :], buf.at[1, pl.ds(r, 1), :], sem).wait()
        return c

    lax.fori_loop(0, tc, drain, 0)
    w = w_ref[...]
    o_ref[...] = h_ref[...] + w[:, 0:1] * buf[0] + w[:, 1:2] * buf[1]


def moe_combine(h, wts, y, pos1, pos2, *, tc):
    T, D = h.shape
    grid_spec = pltpu.PrefetchScalarGridSpec(
        num_scalar_prefetch=2,
        grid=(T // tc,),
        in_specs=[
            pl.BlockSpec((tc, D), lambda i, p1, p2: (i, 0)),
            pl.BlockSpec((tc, LANES), lambda i, p1, p2: (i, 0)),
            pl.BlockSpec(memory_space=pl.ANY),
        ],
        out_specs=pl.BlockSpec((tc, D), lambda i, p1, p2: (i, 0)),
        scratch_shapes=[pltpu.VMEM((2, tc, D), F32), pltpu.SemaphoreType.DMA(())],
    )
    return pl.pallas_call(
        functools.partial(_combine_kernel, tc=tc), grid_spec=grid_spec,
        out_shape=jax.ShapeDtypeStruct((T, D), F32),
        compiler_params=_params(1), name="moe_combine",
    )(pos1, pos2, h, wts, y)


def _route_tables(ids, T, tm):
    e_flat = jnp.concatenate([ids[:, 0], ids[:, 1]]).astype(jnp.int32)
    onehot = (e_flat[:, None] == jnp.arange(N_EXPERTS, dtype=jnp.int32)[None, :]).astype(jnp.int32)
    csum = jnp.cumsum(onehot, axis=0)
    counts = csum[-1]
    rank = jnp.sum((csum - onehot) * onehot, axis=1)
    padded = ((counts + tm - 1) // tm) * tm
    ends = jnp.cumsum(padded)
    starts = ends - padded
    pos = jnp.sum(onehot * starts[None, :], axis=1) + rank
    n_tiles = (2 * T) // tm + N_EXPERTS
    n_valid = ends[-1] // tm
    tile_start = jnp.arange(n_tiles, dtype=jnp.int32) * tm
    te = jnp.sum((tile_start[:, None] >= ends[None, :]).astype(jnp.int32), axis=1)
    last_e = jnp.max(jnp.where(counts > 0, jnp.arange(N_EXPERTS, dtype=jnp.int32), 0))
    te = jnp.minimum(te, last_e)
    tok = jnp.arange(T, dtype=jnp.int32)
    src = jnp.zeros((n_tiles * tm,), jnp.int32).at[pos].set(jnp.concatenate([tok, tok]))
    return src, pos[:T], pos[T:], te, n_valid.reshape(1).astype(jnp.int32)


def _rope_table_kernel(pos_ref, freq_ref, sign_ref, cos_ref, sin_ref):
    ang = pos_ref[...] * freq_ref[...]
    valid = sign_ref[...] != 0.0
    cos_ref[...] = jnp.where(valid, jnp.cos(ang), 0.0)
    sin_ref[...] = jnp.sin(ang) * sign_ref[...]


def rope_tables(pos_col, *, tm):
    T = pos_col.shape[0]
    inv_freq = jnp.exp(-math.log(ROPE_THETA) * jnp.arange(ROPE_HALF, dtype=F32) / ROPE_HALF)
    zeros = jnp.zeros((LANES - ROPE_DIM,), F32)
    freq = jnp.concatenate([inv_freq, inv_freq, zeros]).reshape(1, LANES)
    sign = jnp.concatenate([-jnp.ones((ROPE_HALF,), F32), jnp.ones((ROPE_HALF,), F32),
                            zeros]).reshape(1, LANES)
    row = pl.BlockSpec((1, LANES), lambda i: (0, 0))
    tab = pl.BlockSpec((tm, LANES), lambda i: (i, 0))
    return pl.pallas_call(
        _rope_table_kernel, grid=(T // tm,),
        in_specs=[pl.BlockSpec((tm, 1), lambda i: (i, 0)), row, row],
        out_specs=[tab, tab],
        out_shape=[jax.ShapeDtypeStruct((T, LANES), F32)] * 2,
        compiler_params=_params(1), name="rope_tables",
    )(pos_col, freq, sign)


def _mla_norm_rope(a0, a1, gain, cos, sin, scale):
    ss = jnp.sum(a0 * a0, axis=-1, keepdims=True) + jnp.sum(a1 * a1, axis=-1, keepdims=True)
    r = lax.rsqrt(ss * (1.0 / QK_DIM_B) + EPS)
    y0 = a0 * r * gain[:, :NOPE_DIM]
    y1 = a1 * r * gain[:, NOPE_DIM:]
    lane = lax.broadcasted_iota(jnp.int32, y1.shape, 1)
    partner = jnp.where(lane < ROPE_HALF, pltpu.roll(y1, LANES - ROPE_HALF, 1),
                        pltpu.roll(y1, ROPE_HALF, 1))
    y1 = y1 * cos + partner * sin
    return y0 * scale, y1 * scale


def _latent_kv_kernel(h_ref, ng_ref, wc_ref, wr_ref, cg_ref, c_ref, kr_ref):
    hn = _rms(h_ref[...], ng_ref[...]).astype(BF16)
    c = jnp.dot(hn, wc_ref[...], preferred_element_type=F32)
    c_ref[...] = _rms(c, cg_ref[...]).astype(c_ref.dtype)
    kr_ref[...] = jnp.dot(hn, wr_ref[...], preferred_element_type=F32)


def latent_kv(h, s_norm, wc, wr, c_gain, *, tm):
    T, D = h.shape
    return pl.pallas_call(
        _latent_kv_kernel, grid=(T // tm,),
        in_specs=[
            pl.BlockSpec((tm, D), lambda i: (i, 0)),
            pl.BlockSpec((1, D), lambda i: (0, 0)),
            pl.BlockSpec((D, KV_LORA), lambda i: (0, 0)),
            pl.BlockSpec((D, LANES), lambda i: (0, 0)),
            pl.BlockSpec((1, KV_LORA), lambda i: (0, 0)),
        ],
        out_specs=[pl.BlockSpec((tm, KV_LORA), lambda i: (i, 0)),
                   pl.BlockSpec((tm, LANES), lambda i: (i, 0))],
        out_shape=[jax.ShapeDtypeStruct((T, KV_LORA), BF16),
                   jax.ShapeDtypeStruct((T, LANES), F32)],
        compiler_params=_params(1), name="latent_kv",
    )(h, s_norm.reshape(1, D), wc, wr, c_gain.reshape(1, KV_LORA))


def _kv_up_kernel(c_ref, wk_ref, wv_ref, kr_ref, cos_ref, sin_ref, kg_ref, k_ref, v_ref, *, hpt):
    c = c_ref[...]
    v_ref[...] = jnp.dot(c, wv_ref[...], preferred_element_type=F32).astype(v_ref.dtype)
    kn = jnp.dot(c, wk_ref[...], preferred_element_type=F32)
    kr, cos, sin, gain = kr_ref[...], cos_ref[...], sin_ref[...], kg_ref[...]
    for hh in range(hpt):
        y0, y1 = _mla_norm_rope(kn[:, hh * NOPE_DIM:(hh + 1) * NOPE_DIM], kr, gain, cos, sin, 1.0)
        k_ref[:, hh * MLA_PAD:hh * MLA_PAD + NOPE_DIM] = y0.astype(k_ref.dtype)
        k_ref[:, hh * MLA_PAD + NOPE_DIM:(hh + 1) * MLA_PAD] = y1.astype(k_ref.dtype)


def kv_up(c_kv, wk, wv, k_rope, cos, sin, k_gain_pad, *, tm, hpt):
    T = c_kv.shape[0]
    tab = pl.BlockSpec((tm, LANES), lambda i, j: (i, 0))
    return pl.pallas_call(
        functools.partial(_kv_up_kernel, hpt=hpt),
        grid=(T // tm, N_HEADS // hpt),
        in_specs=[
            pl.BlockSpec((tm, KV_LORA), lambda i, j: (i, 0)),
            pl.BlockSpec((KV_LORA, hpt * NOPE_DIM), lambda i, j: (0, j)),
            pl.BlockSpec((KV_LORA, hpt * HEAD_DIM), lambda i, j: (0, j)),
            tab, tab, tab,
            pl.BlockSpec((1, MLA_PAD), lambda i, j: (0, 0)),
        ],
        out_specs=[pl.BlockSpec((tm, hpt * MLA_PAD), lambda i, j: (i, j)),
                   pl.BlockSpec((tm, hpt * HEAD_DIM), lambda i, j: (i, j))],
        out_shape=[jax.ShapeDtypeStruct((T, N_HEADS * MLA_PAD), BF16),
                   jax.ShapeDtypeStruct((T, N_HEADS * HEAD_DIM), BF16)],
        compiler_params=_params(2), name="kv_up",
    )(c_kv, wk, wv, k_rope, cos, sin, k_gain_pad)


def _q_up_kernel(c_ref, w_ref, cos_ref, sin_ref, qg_ref, q_ref, *, hpt):
    acc = jnp.dot(c_ref[...], w_ref[...], preferred_element_type=F32)
    cos, sin, gain = cos_ref[...], sin_ref[...], qg_ref[...]
    for hh in range(hpt):
        lo = hh * MLA_PAD
        y0, y1 = _mla_norm_rope(acc[:, lo:lo + NOPE_DIM], acc[:, lo + NOPE_DIM:lo + MLA_PAD],
                                gain, cos, sin, QK_DIM_B ** -0.5)
        q_ref[:, lo:lo + NOPE_DIM] = y0.astype(q_ref.dtype)
        q_ref[:, lo + NOPE_DIM:lo + MLA_PAD] = y1.astype(q_ref.dtype)


def q_up(proj_b, w_uq_pad, cos, sin, q_gain_pad, *, tm, hpt):
    T = proj_b.shape[0]
    tab = pl.BlockSpec((tm, LANES), lambda i, j: (i, 0))
    return pl.pallas_call(
        functools.partial(_q_up_kernel, hpt=hpt),
        grid=(T // tm, N_HEADS // hpt),
        in_specs=[
            pl.BlockSpec((tm, Q_LORA), lambda i, j: (i, 0)),
            pl.BlockSpec((Q_LORA, hpt * MLA_PAD), lambda i, j: (0, j)),
            tab, tab,
            pl.BlockSpec((1, MLA_PAD), lambda i, j: (0, 0)),
        ],
        out_specs=pl.BlockSpec((tm, hpt * MLA_PAD), lambda i, j: (i, j)),
        out_shape=jax.ShapeDtypeStruct((T, N_HEADS * MLA_PAD), BF16),
        compiler_params=_params(2), name="q_up",
    )(proj_b, w_uq_pad, cos, sin, q_gain_pad)


def _pad_cols(w, n):
    return jnp.pad(w, ((0, 0), (0, n - w.shape[1])))


def _pad_heads(w, real, padded):
    K = w.shape[0]
    w = w.reshape(K, N_HEADS, real)
    return jnp.pad(w, ((0, 0), (0, 0), (0, padded - real))).reshape(K, N_HEADS * padded)


def kernel(x, mem, positions, a_norm, a_w_in, a_b_forget, a_q_gain, a_k_gain, a_w_out, b_norm, b_w_in, b_q_latent_norm, b_w_uq, b_q_gain, b_w_out, s_norm, s_w_dkv, s_kv_latent_norm, s_w_ukv, s_k_gain, m_norm, m_w_kv, m_q_gain, m_k_gain, f_norm, d_w_gate, d_w_up, d_w_down, e_router, e_w_gate, e_w_up, e_w_down):
    B, S, D = x.shape
    M = mem.shape[1]
    T = B * S
    depth = f_norm.shape[0]
    n_a = a_w_in.shape[0]
    d_a = N_HEADS * HEAD_DIM
    d_memq = N_HEADS_MEM * HEAD_DIM

    tm = min(512, T)
    tq = min(512, S)
    tk = min(512, S)
    tq_mem = min(1024, S)
    tm_e = min(512, T)
    tc = min(256, T)
    tf_dense = 512
    tf_moe = 512

    h = x.reshape(T, D)
    mkv = mem_kv(mem.reshape(B * M, D), m_norm, m_w_kv.astype(BF16), m_k_gain)
    q_scale = HEAD_DIM ** -0.5
    cos = sin = shared_k = shared_v = None

    for l in range(depth):
        memq_gain = jnp.tile(m_q_gain[l] * q_scale, N_HEADS_MEM)
        if l < n_a:
            w_in = a_w_in[l]
            w_main = jnp.concatenate([w_in[:, :4 * d_a], w_in[:, 4 * d_a + N_HEADS:]], axis=1).astype(BF16)
            w_f = _pad_cols(w_in[:, 4 * d_a:4 * d_a + N_HEADS], LANES).astype(BF16)
            colgain = jnp.concatenate([
                jnp.tile(a_q_gain[l] * q_scale, N_HEADS), jnp.tile(a_k_gain[l], N_HEADS),
                jnp.ones((2 * d_a,), F32), memq_gain])
            tn = 512
            nt = d_a // tn
            modes = ((0, 2 * nt, HEAD_DIM), (2 * nt, 4 * nt, 0), (4 * nt, 4 * nt + d_memq // tn, HEAD_DIM))
            proj, f_logit = norm_proj(h, a_norm[l], w_main, colgain, modes, tm=tm, tn=tn, wf=w_f)
            f_t = f_logit[:, :N_HEADS].reshape(B, S, N_HEADS).transpose(0, 2, 1).reshape(B * N_HEADS, S)
            bias = jnp.tile(a_b_forget[l], B).reshape(B * N_HEADS, 1)
            f_cum = forget_scan(f_t, bias).reshape(B * N_HEADS, 1, S)
            mix = causal_attention(proj, 0, proj, N_HEADS, proj, 2 * N_HEADS, B=B, S=S,
                                   dk=HEAD_DIM, dv=HEAD_DIM, tq=tq, tk=tk,
                                   f_cum=f_cum, g_arr=proj, g_col0=3 * N_HEADS)
            mo = mem_attention(proj, 4 * N_HEADS, mkv, l, B=B, S=S, M=M, tq=tq_mem)
            w_out = a_w_out[l]
        else:
            jb = l - n_a
            if shared_k is None:
                cos, sin = rope_tables(positions.reshape(T, 1).astype(F32), tm=tm)
                c_kv, k_rope = latent_kv(h, s_norm, s_w_dkv[:, :KV_LORA].astype(BF16),
                                         _pad_cols(s_w_dkv[:, KV_LORA:], LANES).astype(BF16),
                                         s_kv_latent_norm, tm=tm)
                w_ukv = s_w_ukv.reshape(KV_LORA, N_HEADS, NOPE_DIM + HEAD_DIM)
                wk = w_ukv[:, :, :NOPE_DIM].reshape(KV_LORA, N_HEADS * NOPE_DIM).astype(BF16)
                wv = w_ukv[:, :, NOPE_DIM:].reshape(KV_LORA, N_HEADS * HEAD_DIM).astype(BF16)
                k_gain_pad = jnp.pad(s_k_gain, (0, MLA_PAD - QK_DIM_B)).reshape(1, MLA_PAD)
                shared_k, shared_v = kv_up(c_kv, wk, wv, k_rope, cos, sin, k_gain_pad, tm=tm, hpt=4)
            colgain = jnp.concatenate([b_q_latent_norm[jb], memq_gain])
            modes = ((0, 1, Q_LORA), (1, 2, HEAD_DIM))
            proj = norm_proj(h, b_norm[jb], b_w_in[jb].astype(BF16), colgain, modes, tm=tm, tn=512)
            q_gain_pad = jnp.pad(b_q_gain[jb], (0, MLA_PAD - QK_DIM_B)).reshape(1, MLA_PAD)
            q = q_up(proj, _pad_heads(b_w_uq[jb], QK_DIM_B, MLA_PAD).astype(BF16), cos, sin,
                     q_gain_pad, tm=tm, hpt=2)
            mix = causal_attention(q, 0, shared_k, 0, shared_v, 0, B=B, S=S,
                                   dk=MLA_PAD, dv=HEAD_DIM, tq=tq, tk=tk)
            mo = mem_attention(proj, Q_LORA // HEAD_DIM, mkv, l, B=B, S=S, M=M, tq=tq_mem)
            w_out = b_w_out[jb]

        d_mix = mix.shape[1]
        wa, wm = w_out[:d_mix].astype(BF16), w_out[d_mix:].astype(BF16)
        i = l // 2
        if l % 2 == 0:
            h = out_proj(mix, mo, wa, wm, h, tm=tm)
            h = dense_ffn(h, f_norm[l], d_w_gate[i].astype(BF16), d_w_up[i].astype(BF16),
                          d_w_down[i].astype(BF16), tm=tm, tf=tf_dense)
        else:
            h, ids, wts = out_proj(mix, mo, wa, wm, h, tm=tm, f_gain=f_norm[l],
                                   w_router=_pad_cols(e_router[i], LANES))
            src, pos1, pos2, te, n_valid = _route_tables(ids, T, tm_e)
            x_sorted = gather_rows(h, src, n_valid, tm=tm_e)
            y = moe_ffn(x_sorted, te, n_valid, f_norm[l], e_w_gate[i].astype(BF16),
                        e_w_up[i].astype(BF16), e_w_down[i].astype(BF16), tm=tm_e, tf=tf_moe)
            h = moe_combine(h, wts, y, pos1, pos2, tc=tc)
    return h.reshape(B, S, D)
```

```python
import functools
import math

import jax
import jax.numpy as jnp
from jax import lax
from jax.experimental import pallas as pl
from jax.experimental.pallas import tpu as pltpu

F32 = jnp.float32
BF16 = jnp.bfloat16

EPS = 1e-6
HEAD_DIM = 128
N_HEADS = 12
N_HEADS_MEM = 4
NOPE_DIM = 128
ROPE_DIM = 64
ROPE_HALF = ROPE_DIM // 2
QK_DIM_B = NOPE_DIM + ROPE_DIM
MLA_PAD = 256
Q_LORA = 512
KV_LORA = 512
ROPE_THETA = 10000.0
N_EXPERTS = 8
LANES = 128
LOG2E = math.log2(math.e)
ATTN_HEADS_PER_STEP = 2
VMEM_LIMIT = 56 * 1024 * 1024
MOE_VMEM_LIMIT = 60 * 1024 * 1024


def _params(n_grid, vmem=VMEM_LIMIT):
    return pltpu.CompilerParams(dimension_semantics=("arbitrary",) * n_grid, vmem_limit_bytes=vmem)


def _rms(x, gain):
    ms = jnp.mean(x * x, axis=-1, keepdims=True)
    return x * lax.rsqrt(ms + EPS) * gain


def _norm_proj_kernel(*refs, modes, tn, with_f):
    if with_f:
        h_ref, ng_ref, w_ref, cg_ref, wf_ref, o_ref, f_ref, hn_ref = refs
    else:
        h_ref, ng_ref, w_ref, cg_ref, o_ref, hn_ref = refs
    j = pl.program_id(1)

    @pl.when(j == 0)
    def _():
        hn = _rms(h_ref[...], ng_ref[...]).astype(BF16)
        hn_ref[...] = hn
        if with_f:
            f_ref[...] = jnp.dot(hn, wf_ref[...], preferred_element_type=F32)

    acc = jnp.dot(hn_ref[...], w_ref[...], preferred_element_type=F32)
    for lo, hi, gs in modes:
        @pl.when((j >= lo) & (j < hi))
        def _(gs=gs):
            if gs == 0:
                o_ref[...] = acc.astype(o_ref.dtype)
            else:
                for g in range(tn // gs):
                    sl = slice(g * gs, (g + 1) * gs)
                    o_ref[:, sl] = _rms(acc[:, sl], cg_ref[:, sl]).astype(o_ref.dtype)


def norm_proj(h, ngain, w, colgain, modes, *, tm, tn, wf=None):
    T, D = h.shape
    N = w.shape[1]
    with_f = wf is not None
    in_specs = [
        pl.BlockSpec((tm, D), lambda i, j: (i, 0)),
        pl.BlockSpec((1, D), lambda i, j: (0, 0)),
        pl.BlockSpec((D, tn), lambda i, j: (0, j)),
        pl.BlockSpec((1, tn), lambda i, j: (0, j)),
    ]
    args = [h, ngain.reshape(1, D), w, colgain.reshape(1, N)]
    out_shape = [jax.ShapeDtypeStruct((T, N), BF16)]
    out_specs = [pl.BlockSpec((tm, tn), lambda i, j: (i, j))]
    if with_f:
        in_specs.append(pl.BlockSpec((D, LANES), lambda i, j: (0, 0)))
        args.append(wf)
        out_shape.append(jax.ShapeDtypeStruct((T, LANES), F32))
        out_specs.append(pl.BlockSpec((tm, LANES), lambda i, j: (i, 0)))
    outs = pl.pallas_call(
        functools.partial(_norm_proj_kernel, modes=modes, tn=tn, with_f=with_f),
        grid=(T // tm, N // tn),
        in_specs=in_specs, out_specs=out_specs, out_shape=out_shape,
        scratch_shapes=[pltpu.VMEM((tm, D), BF16)],
        compiler_params=_params(2), name="norm_proj",
    )(*args)
    return outs if with_f else outs[0]


def _forget_scan_kernel(x_ref, b_ref, tri_ref, o_ref, *, n_chunks):
    z = x_ref[...] + b_ref[...]
    log_f = jnp.minimum(z, 0.0) - jnp.log1p(jnp.exp(-jnp.abs(z)))
    tri = tri_ref[...]
    carry = jnp.zeros((z.shape[0], 1), F32)
    for c in range(n_chunks):
        sl = slice(c * LANES, (c + 1) * LANES)
        cs = jnp.dot(log_f[:, sl], tri, precision=lax.Precision.HIGHEST,
                     preferred_element_type=F32) + carry
        o_ref[:, sl] = cs * LOG2E
        carry = cs[:, LANES - 1:LANES]


def forget_scan(f_logit_t, bias_col):
    R, S = f_logit_t.shape
    tri = (lax.broadcasted_iota(jnp.int32, (LANES, LANES), 0)
           <= lax.broadcasted_iota(jnp.int32, (LANES, LANES), 1)).astype(F32)
    return pl.pallas_call(
        functools.partial(_forget_scan_kernel, n_chunks=S // LANES),
        out_shape=jax.ShapeDtypeStruct((R, S), F32), name="forget_scan",
    )(f_logit_t, bias_col, tri)


def _flash_kernel(*refs, tq, tk, dk, dv, hp, has_f, has_g):
    it = iter(refs)
    q_ref, k_ref, v_ref = next(it), next(it), next(it)
    f_ref = next(it) if has_f else None
    g_ref = next(it) if has_g else None
    o_ref, m_ref, l_ref, acc_ref = next(it), next(it), next(it), next(it)
    i = pl.program_id(2)
    m_ref[...] = jnp.full(m_ref.shape, -jnp.inf, F32)
    l_ref[...] = jnp.zeros(l_ref.shape, F32)
    acc_ref[...] = jnp.zeros(acc_ref.shape, F32)
    rep = tk // LANES

    def step(j, masked):
        off = pl.multiple_of(j * tk, tk)
        for hh in range(hp):
            s = lax.dot_general(q_ref[:, hh * dk:(hh + 1) * dk],
                                k_ref[pl.ds(off, tk), hh * dk:(hh + 1) * dk],
                                (((1,), (1,)), ((), ())), preferred_element_type=F32)
            if has_f:
                s = s - f_ref[hh, :, pl.ds(off, tk)]
            if masked:
                row = i * tq + lax.broadcasted_iota(jnp.int32, (tq, tk), 0)
                col = off + lax.broadcasted_iota(jnp.int32, (tq, tk), 1)
                s = jnp.where(col <= row, s, -jnp.inf)
            m_prev = m_ref[hh]
            m_new = jnp.maximum(m_prev, jnp.max(s, axis=-1, keepdims=True))
            alpha = jnp.exp2(m_prev - m_new)
            p = jnp.exp2(s - jnp.concatenate([m_new] * rep, axis=1))
            l_ref[hh] = alpha * l_ref[hh] + jnp.sum(p, axis=-1, keepdims=True)
            acc_ref[hh] = alpha * acc_ref[hh] + jnp.dot(
                p.astype(BF16), v_ref[pl.ds(off, tk), hh * dv:(hh + 1) * dv],
                preferred_element_type=F32)
            m_ref[hh] = m_new

    r = tq // tk

    def full_step(j, c):
        step(j, False)
        return c

    lax.fori_loop(0, i * r, full_step, 0)
    for d in range(r):
        step(i * r + d, True)
    for hh in range(hp):
        o = acc_ref[hh] / l_ref[hh]
        if has_g:
            o = o * jax.nn.sigmoid(g_ref[:, hh * dv:(hh + 1) * dv].astype(F32))
        o_ref[:, hh * dv:(hh + 1) * dv] = o.astype(o_ref.dtype)


def causal_attention(q_arr, q_col0, k_arr, k_col0, v_arr, v_col0, *, B, S, dk, dv, tq, tk,
                     f_cum=None, g_arr=None, g_col0=0):
    hp = ATTN_HEADS_PER_STEP
    assert dv == LANES and tq % tk == 0
    assert q_col0 % hp == 0 and k_col0 % hp == 0 and v_col0 % hp == 0 and g_col0 % hp == 0
    nq = S // tq
    T = B * S
    has_f, has_g = f_cum is not None, g_arr is not None
    in_specs = [
        pl.BlockSpec((tq, hp * dk), lambda b, h, i: (b * nq + i, q_col0 // hp + h)),
        pl.BlockSpec((S, hp * dk), lambda b, h, i: (b, k_col0 // hp + h)),
        pl.BlockSpec((S, hp * dv), lambda b, h, i: (b, v_col0 // hp + h)),
    ]
    args = [q_arr, k_arr, v_arr]
    if has_f:
        in_specs.append(pl.BlockSpec((hp, 1, S), lambda b, h, i: (b * (N_HEADS // hp) + h, 0, 0)))
        args.append(f_cum)
    if has_g:
        in_specs.append(pl.BlockSpec((tq, hp * dv), lambda b, h, i: (b * nq + i, g_col0 // hp + h)))
        args.append(g_arr)
    return pl.pallas_call(
        functools.partial(_flash_kernel, tq=tq, tk=tk, dk=dk, dv=dv, hp=hp, has_f=has_f, has_g=has_g),
        grid=(B, N_HEADS // hp, nq),
        in_specs=in_specs,
        out_specs=pl.BlockSpec((tq, hp * dv), lambda b, h, i: (b * nq + i, h)),
        out_shape=jax.ShapeDtypeStruct((T, N_HEADS * dv), BF16),
        scratch_shapes=[pltpu.VMEM((hp, tq, LANES), F32), pltpu.VMEM((hp, tq, LANES), F32),
                        pltpu.VMEM((hp, tq, dv), F32)],
        compiler_params=_params(3), name="causal_attention",
    )(*args)


def _mem_kv_kernel(mem_ref, ng_ref, w_ref, kg_ref, o_ref, *, n_norm_cols):
    x = _rms(mem_ref[...], ng_ref[...]).astype(BF16)
    acc = jnp.dot(x, w_ref[...], preferred_element_type=F32)
    for g in range(acc.shape[1] // HEAD_DIM):
        sl = slice(g * HEAD_DIM, (g + 1) * HEAD_DIM)
        if g * HEAD_DIM < n_norm_cols:
            o_ref[:, sl] = _rms(acc[:, sl], kg_ref[...]).astype(o_ref.dtype)
        else:
            o_ref[:, sl] = acc[:, sl].astype(o_ref.dtype)


def mem_kv(mem2d, m_norm, m_w_kv, m_k_gain):
    L, D, N = m_w_kv.shape
    R = mem2d.shape[0]
    return pl.pallas_call(
        functools.partial(_mem_kv_kernel, n_norm_cols=N // 2),
        grid=(L,),
        in_specs=[
            pl.BlockSpec((R, D), lambda l: (0, 0)),
            pl.BlockSpec((None, 1, D), lambda l: (l, 0, 0)),
            pl.BlockSpec((None, D, N), lambda l: (l, 0, 0)),
            pl.BlockSpec((None, 1, HEAD_DIM), lambda l: (l, 0, 0)),
        ],
        out_specs=pl.BlockSpec((None, R, N), lambda l: (l, 0, 0)),
        out_shape=jax.ShapeDtypeStruct((L, R, N), BF16),
        compiler_params=_params(1), name="mem_kv",
    )(mem2d, m_norm.reshape(L, 1, D), m_w_kv, m_k_gain.reshape(L, 1, HEAD_DIM))


def _mem_attn_kernel(q_ref, k_ref, v_ref, o_ref):
    s = lax.dot_general(q_ref[...], k_ref[...], (((1,), (1,)), ((), ())),
                        preferred_element_type=F32)
    m = jnp.max(s, axis=-1, keepdims=True)
    p = jnp.exp(s - m)
    l = jnp.sum(p, axis=-1, keepdims=True)
    o = jnp.dot(p.astype(BF16), v_ref[...], preferred_element_type=F32) / l
    o_ref[...] = o.astype(o_ref.dtype)


def mem_attention(q_arr, q_col0, mkv, layer, *, B, S, M, tq):
    nq = S // tq
    T = B * S
    return pl.pallas_call(
        _mem_attn_kernel,
        grid=(B, N_HEADS_MEM, nq),
        in_specs=[
            pl.BlockSpec((tq, HEAD_DIM), lambda b, h, i: (b * nq + i, q_col0 + h)),
            pl.BlockSpec((None, M, HEAD_DIM), lambda b, h, i: (layer, b, h)),
            pl.BlockSpec((None, M, HEAD_DIM), lambda b, h, i: (layer, b, N_HEADS_MEM + h)),
        ],
        out_specs=pl.BlockSpec((tq, HEAD_DIM), lambda b, h, i: (b * nq + i, h)),
        out_shape=jax.ShapeDtypeStruct((T, N_HEADS_MEM * HEAD_DIM), BF16),
        compiler_params=_params(3), name="mem_attention",
    )(q_arr, mkv, mkv)


def _out_proj_kernel(*refs, with_router):
    if with_router:
        xa_ref, xm_ref, wa_ref, wm_ref, h_ref, fg_ref, wr_ref, o_ref, ids_ref, wts_ref = refs
    else:
        xa_ref, xm_ref, wa_ref, wm_ref, h_ref, o_ref = refs
    acc = jnp.dot(xa_ref[...], wa_ref[...], preferred_element_type=F32)
    acc = acc + jnp.dot(xm_ref[...], wm_ref[...], preferred_element_type=F32)
    h_new = h_ref[...] + acc
    o_ref[...] = h_new
    if with_router:
        hn = _rms(h_new, fg_ref[...])
        hn_hi = hn.astype(BF16)
        hn_lo = (hn - hn_hi.astype(F32)).astype(BF16)
        wr = wr_ref[...]
        wr_hi = wr.astype(BF16)
        wr_lo = (wr - wr_hi.astype(F32)).astype(BF16)
        logits = (jnp.dot(hn_hi, wr_hi, preferred_element_type=F32)
                  + jnp.dot(hn_hi, wr_lo, preferred_element_type=F32)
                  + jnp.dot(hn_lo, wr_hi, preferred_element_type=F32))
        lane = lax.broadcasted_iota(jnp.int32, logits.shape, 1)
        lane_f = lane.astype(F32)
        logits = jnp.where(lane < N_EXPERTS, logits, -jnp.inf)
        l1 = jnp.max(logits, axis=-1, keepdims=True)
        i1 = jnp.min(jnp.where(logits == l1, lane_f, float(LANES)), axis=-1, keepdims=True)
        rest = jnp.where(lane_f == i1, -jnp.inf, logits)
        l2 = jnp.max(rest, axis=-1, keepdims=True)
        i2 = jnp.min(jnp.where(rest == l2, lane_f, float(LANES)), axis=-1, keepdims=True)
        e = jnp.exp(l2 - l1)
        w1 = 1.0 / (1.0 + e)
        w2 = e / (1.0 + e)
        ids_ref[...] = jnp.where(lane == 0, i1, jnp.where(lane == 1, i2, 0.0))
        wts_ref[...] = jnp.where(lane == 0, w1, jnp.where(lane == 1, w2, 0.0))


def out_proj(xa, xm, wa, wm, h, *, tm, f_gain=None, w_router=None):
    T, D = h.shape
    Ka, Km = xa.shape[1], xm.shape[1]
    with_router = w_router is not None
    in_specs = [
        pl.BlockSpec((tm, Ka), lambda i: (i, 0)),
        pl.BlockSpec((tm, Km), lambda i: (i, 0)),
        pl.BlockSpec((Ka, D), lambda i: (0, 0)),
        pl.BlockSpec((Km, D), lambda i: (0, 0)),
        pl.BlockSpec((tm, D), lambda i: (i, 0)),
    ]
    args = [xa, xm, wa, wm, h]
    out_shape = [jax.ShapeDtypeStruct((T, D), F32)]
    out_specs = [pl.BlockSpec((tm, D), lambda i: (i, 0))]
    if with_router:
        in_specs += [pl.BlockSpec((1, D), lambda i: (0, 0)),
                     pl.BlockSpec((D, LANES), lambda i: (0, 0))]
        args += [f_gain.reshape(1, D), w_router]
        out_shape += [jax.ShapeDtypeStruct((T, LANES), F32)] * 2
        out_specs += [pl.BlockSpec((tm, LANES), lambda i: (i, 0))] * 2
    outs = pl.pallas_call(
        functools.partial(_out_proj_kernel, with_router=with_router),
        grid=(T // tm,),
        in_specs=in_specs, out_specs=out_specs, out_shape=out_shape,
        compiler_params=_params(1), name="out_proj",
    )(*args)
    return outs if with_router else outs[0]


def _swiglu_step(hn, wg_ref, wu_ref, wd_ref):
    g = jnp.dot(hn, wg_ref[...], preferred_element_type=F32)
    u = jnp.dot(hn, wu_ref[...], preferred_element_type=F32)
    a = (g * jax.nn.sigmoid(g)) * u
    return jnp.dot(a.astype(BF16), wd_ref[...], preferred_element_type=F32)


def _dense_ffn_kernel(h_ref, fg_ref, wg_ref, wu_ref, wd_ref, o_ref, hn_ref, acc_ref):
    f = pl.program_id(1)

    @pl.when(f == 0)
    def _():
        hn_ref[...] = _rms(h_ref[...], fg_ref[...]).astype(BF16)
        acc_ref[...] = jnp.zeros(acc_ref.shape, F32)

    acc_ref[...] += _swiglu_step(hn_ref[...], wg_ref, wu_ref, wd_ref)

    @pl.when(f == pl.num_programs(1) - 1)
    def _():
        o_ref[...] = h_ref[...] + acc_ref[...]


def dense_ffn(h, f_gain, wg, wu, wd, *, tm, tf):
    T, D = h.shape
    FF = wg.shape[1]
    return pl.pallas_call(
        _dense_ffn_kernel,
        grid=(T // tm, FF // tf),
        in_specs=[
            pl.BlockSpec((tm, D), lambda i, f: (i, 0)),
            pl.BlockSpec((1, D), lambda i, f: (0, 0)),
            pl.BlockSpec((D, tf), lambda i, f: (0, f)),
            pl.BlockSpec((D, tf), lambda i, f: (0, f)),
            pl.BlockSpec((tf, D), lambda i, f: (f, 0)),
        ],
        out_specs=pl.BlockSpec((tm, D), lambda i, f: (i, 0)),
        out_shape=jax.ShapeDtypeStruct((T, D), F32),
        scratch_shapes=[pltpu.VMEM((tm, D), BF16), pltpu.VMEM((tm, D), F32)],
        compiler_params=_params(2), name="dense_ffn",
    )(h, f_gain.reshape(1, D), wg, wu, wd)


def _moe_ffn_kernel(te_ref, nv_ref, rows_ref, x_ref, wg_ref, wu_ref, wd_ref, y_ref,
                    wgb_ref, wub_ref, wdb_ref, *, sub):
    i, f = pl.program_id(0), pl.program_id(1)
    tm = x_ref.shape[0]

    @pl.when(i < nv_ref[0])
    def _():
        wgb_ref[...] = wg_ref[...].astype(BF16)
        wub_ref[...] = wu_ref[...].astype(BF16)
        wdb_ref[...] = wd_ref[...].astype(BF16)
        for sb in range(tm // sub):
            sl = slice(sb * sub, (sb + 1) * sub)
            live = sb * sub < rows_ref[i]

            @pl.when(live)
            def _(sl=sl):
                d = _swiglu_step(x_ref[sl, :], wgb_ref, wub_ref, wdb_ref)

                @pl.when(f == 0)
                def _():
                    y_ref[sl, :] = d

                @pl.when(f > 0)
                def _():
                    y_ref[sl, :] += d

            @pl.when(jnp.logical_not(live) & (f == 0))
            def _(sl=sl):
                y_ref[sl, :] = jnp.zeros((sub, y_ref.shape[1]), y_ref.dtype)

    @pl.when((i >= nv_ref[0]) & (f == 0))
    def _():
        y_ref[...] = jnp.zeros(y_ref.shape, y_ref.dtype)


def moe_ffn(x_sorted, tile_expert, n_valid, tile_rows, wg, wu, wd, *, tm, tf, sub):
    P, D = x_sorted.shape
    FF = wg.shape[2]
    nf = FF // tf

    def row_map(i, f, te, nv, rows):
        return (jnp.minimum(i, nv[0] - 1), 0)

    def f_idx(i, f, nv):
        return jnp.where(i < nv[0], f, nf - 1)

    grid_spec = pltpu.PrefetchScalarGridSpec(
        num_scalar_prefetch=3,
        grid=(P // tm, nf),
        in_specs=[
            pl.BlockSpec((tm, D), row_map),
            pl.BlockSpec((None, D, tf), lambda i, f, te, nv, rows: (te[i], 0, f_idx(i, f, nv))),
            pl.BlockSpec((None, D, tf), lambda i, f, te, nv, rows: (te[i], 0, f_idx(i, f, nv))),
            pl.BlockSpec((None, tf, D), lambda i, f, te, nv, rows: (te[i], f_idx(i, f, nv), 0)),
        ],
        out_specs=pl.BlockSpec((tm, D), lambda i, f, te, nv, rows: (i, 0),
                               pipeline_mode=pl.Buffered(1)),
        scratch_shapes=[pltpu.VMEM((D, tf), BF16), pltpu.VMEM((D, tf), BF16),
                        pltpu.VMEM((tf, D), BF16)],
    )
    return pl.pallas_call(
        functools.partial(_moe_ffn_kernel, sub=sub), grid_spec=grid_spec,
        out_shape=jax.ShapeDtypeStruct((P, D), F32),
        compiler_params=_params(2, MOE_VMEM_LIMIT), name="moe_ffn",
    )(tile_expert, n_valid, tile_rows, x_sorted, wg, wu, wd)


def _gather_norm_kernel(src_ref, nv_ref, h_any, fg_ref, o_ref, buf, sem, *, tm):
    i = pl.program_id(0)
    n_valid = nv_ref[0]

    def start_tile(tile, slot):
        def issue(r, c):
            tok = src_ref[tile * tm + r]
            pltpu.make_async_copy(h_any.at[pl.ds(tok, 1), :], buf.at[slot, pl.ds(r, 1), :],
                                  sem.at[slot]).start()
            return c

        lax.fori_loop(0, tm, issue, 0)

    @pl.when(i == 0)
    def _():
        start_tile(0, 0)

    @pl.when(i + 1 < n_valid)
    def _():
        start_tile(i + 1, (i + 1) % 2)

    @pl.when(i < n_valid)
    def _():
        slot = i % 2
        pltpu.make_async_copy(h_any.at[pl.ds(0, tm), :], buf.at[slot], sem.at[slot]).wait()
        o_ref[...] = _rms(buf[slot], fg_ref[...]).astype(o_ref.dtype)

    @pl.when(i >= n_valid)
    def _():
        o_ref[...] = jnp.zeros(o_ref.shape, o_ref.dtype)


def gather_norm_rows(h, src_tok, n_valid, f_gain, *, tm):
    T, D = h.shape
    P = src_tok.shape[0]
    grid_spec = pltpu.PrefetchScalarGridSpec(
        num_scalar_prefetch=2,
        grid=(P // tm,),
        in_specs=[pl.BlockSpec(memory_space=pl.ANY),
                  pl.BlockSpec((1, D), lambda i, src, nv: (0, 0))],
        out_specs=pl.BlockSpec((tm, D), lambda i, src, nv: (i, 0)),
        scratch_shapes=[pltpu.VMEM((2, tm, D), F32), pltpu.SemaphoreType.DMA((2,))],
    )
    return pl.pallas_call(
        functools.partial(_gather_norm_kernel, tm=tm), grid_spec=grid_spec,
        out_shape=jax.ShapeDtypeStruct((P, D), BF16),
        compiler_params=_params(1), name="gather_norm_rows",
    )(src_tok, n_valid, h, f_gain.reshape(1, D))


def _combine_kernel(p1_ref, p2_ref, h_ref, w_ref, y_any, o_ref, buf, sem, *, tc):
    i = pl.program_id(0)

    def start_tile(tile, slot):
        def issue(r, c):
            pltpu.make_async_copy(y_any.at[pl.ds(p1_ref[tile * tc + r], 1), :],
                                  buf.at[slot, 0, pl.ds(r, 1), :], sem.at[slot]).start()
            pltpu.make_async_copy(y_any.at[pl.ds(p2_ref[tile * tc + r], 1), :],
                                  buf.at[slot, 1, pl.ds(r, 1), :], sem.at[slot]).start()
            return c

        lax.fori_loop(0, tc, issue, 0)

    @pl.when(i == 0)
    def _():
        start_tile(0, 0)

    @pl.when(i + 1 < pl.num_programs(0))
    def _():
        start_tile(i + 1, (i + 1) % 2)

    slot = i % 2
    for k in range(2):
        pltpu.make_async_copy(y_any.at[pl.ds(0, tc), :], buf.at[slot, k], sem.at[slot]).wait()
    w = w_ref[...]
    o_ref[...] = h_ref[...] + w[:, 0:1] * buf[slot, 0] + w[:, 1:2] * buf[slot, 1]


def moe_combine(h, wts, y, pos1, pos2, *, tc):
    T, D = h.shape
    grid_spec = pltpu.PrefetchScalarGridSpec(
        num_scalar_prefetch=2,
        grid=(T // tc,),
        in_specs=[
            pl.BlockSpec((tc, D), lambda i, p1, p2: (i, 0)),
            pl.BlockSpec((tc, LANES), lambda i, p1, p2: (i, 0)),
            pl.BlockSpec(memory_space=pl.ANY),
        ],
        out_specs=pl.BlockSpec((tc, D), lambda i, p1, p2: (i, 0)),
        scratch_shapes=[pltpu.VMEM((2, 2, tc, D), F32), pltpu.SemaphoreType.DMA((2,))],
    )
    return pl.pallas_call(
        functools.partial(_combine_kernel, tc=tc), grid_spec=grid_spec,
        out_shape=jax.ShapeDtypeStruct((T, D), F32),
        compiler_params=_params(1), name="moe_combine",
    )(pos1, pos2, h, wts, y)


def _route_tables(ids, T, tm):
    e_flat = jnp.concatenate([ids[:, 0], ids[:, 1]]).astype(jnp.int32)
    onehot = (e_flat[:, None] == jnp.arange(N_EXPERTS, dtype=jnp.int32)[None, :]).astype(jnp.int32)
    csum = jnp.cumsum(onehot, axis=0)
    counts = csum[-1]
    rank = jnp.sum((csum - onehot) * onehot, axis=1)
    padded = ((counts + tm - 1) // tm) * tm
    ends = jnp.cumsum(padded)
    starts = ends - padded
    pos = jnp.sum(onehot * starts[None, :], axis=1) + rank
    n_tiles = (2 * T) // tm + N_EXPERTS
    n_valid = ends[-1] // tm
    tile_start = jnp.arange(n_tiles, dtype=jnp.int32) * tm
    te = jnp.sum((tile_start[:, None] >= ends[None, :]).astype(jnp.int32), axis=1)
    last_e = jnp.max(jnp.where(counts > 0, jnp.arange(N_EXPERTS, dtype=jnp.int32), 0))
    te = jnp.minimum(te, last_e)
    tile_rows = jnp.clip(counts[te] - (tile_start - starts[te]), 0, tm)
    tile_rows = jnp.where(tile_start < ends[-1], tile_rows, 0).astype(jnp.int32)
    tok = jnp.arange(T, dtype=jnp.int32)
    src = jnp.zeros((n_tiles * tm,), jnp.int32).at[pos].set(jnp.concatenate([tok, tok]))
    return src, pos[:T], pos[T:], te, n_valid.reshape(1).astype(jnp.int32), tile_rows


def _rope_table_kernel(pos_ref, freq_ref, sign_ref, cos_ref, sin_ref):
    ang = pos_ref[...] * freq_ref[...]
    valid = sign_ref[...] != 0.0
    cos_ref[...] = jnp.where(valid, jnp.cos(ang), 0.0)
    sin_ref[...] = jnp.sin(ang) * sign_ref[...]


def rope_tables(pos_col, *, tm):
    T = pos_col.shape[0]
    inv_freq = jnp.exp(-math.log(ROPE_THETA) * jnp.arange(ROPE_HALF, dtype=F32) / ROPE_HALF)
    zeros = jnp.zeros((LANES - ROPE_DIM,), F32)
    freq = jnp.concatenate([inv_freq, inv_freq, zeros]).reshape(1, LANES)
    sign = jnp.concatenate([-jnp.ones((ROPE_HALF,), F32), jnp.ones((ROPE_HALF,), F32),
                            zeros]).reshape(1, LANES)
    row = pl.BlockSpec((1, LANES), lambda i: (0, 0))
    tab = pl.BlockSpec((tm, LANES), lambda i: (i, 0))
    return pl.pallas_call(
        _rope_table_kernel, grid=(T // tm,),
        in_specs=[pl.BlockSpec((tm, 1), lambda i: (i, 0)), row, row],
        out_specs=[tab, tab],
        out_shape=[jax.ShapeDtypeStruct((T, LANES), F32)] * 2,
        compiler_params=_params(1), name="rope_tables",
    )(pos_col, freq, sign)


def _mla_norm_rope(a0, a1, gain, cos, sin, scale):
    ss = jnp.sum(a0 * a0, axis=-1, keepdims=True) + jnp.sum(a1 * a1, axis=-1, keepdims=True)
    r = lax.rsqrt(ss * (1.0 / QK_DIM_B) + EPS)
    y0 = a0 * r * gain[:, :NOPE_DIM]
    y1 = a1 * r * gain[:, NOPE_DIM:]
    lane = lax.broadcasted_iota(jnp.int32, y1.shape, 1)
    partner = jnp.where(lane < ROPE_HALF, pltpu.roll(y1, LANES - ROPE_HALF, 1),
                        pltpu.roll(y1, ROPE_HALF, 1))
    y1 = y1 * cos + partner * sin
    return y0 * scale, y1 * scale


def _latent_kv_kernel(h_ref, ng_ref, wc_ref, wr_ref, cg_ref, c_ref, kr_ref):
    hn = _rms(h_ref[...], ng_ref[...]).astype(BF16)
    c = jnp.dot(hn, wc_ref[...], preferred_element_type=F32)
    c_ref[...] = _rms(c, cg_ref[...]).astype(c_ref.dtype)
    kr_ref[...] = jnp.dot(hn, wr_ref[...], preferred_element_type=F32)


def latent_kv(h, s_norm, wc, wr, c_gain, *, tm):
    T, D = h.shape
    return pl.pallas_call(
        _latent_kv_kernel, grid=(T // tm,),
        in_specs=[
            pl.BlockSpec((tm, D), lambda i: (i, 0)),
            pl.BlockSpec((1, D), lambda i: (0, 0)),
            pl.BlockSpec((D, KV_LORA), lambda i: (0, 0)),
            pl.BlockSpec((D, LANES), lambda i: (0, 0)),
            pl.BlockSpec((1, KV_LORA), lambda i: (0, 0)),
        ],
        out_specs=[pl.BlockSpec((tm, KV_LORA), lambda i: (i, 0)),
                   pl.BlockSpec((tm, LANES), lambda i: (i, 0))],
        out_shape=[jax.ShapeDtypeStruct((T, KV_LORA), BF16),
                   jax.ShapeDtypeStruct((T, LANES), F32)],
        compiler_params=_params(1), name="latent_kv",
    )(h, s_norm.reshape(1, D), wc, wr, c_gain.reshape(1, KV_LORA))


def _kv_up_kernel(c_ref, wk_ref, wv_ref, kr_ref, cos_ref, sin_ref, kg_ref, k_ref, v_ref, *, hpt):
    c = c_ref[...]
    v_ref[...] = jnp.dot(c, wv_ref[...], preferred_element_type=F32).astype(v_ref.dtype)
    kn = jnp.dot(c, wk_ref[...], preferred_element_type=F32)
    kr, cos, sin, gain = kr_ref[...], cos_ref[...], sin_ref[...], kg_ref[...]
    for hh in range(hpt):
        y0, y1 = _mla_norm_rope(kn[:, hh * NOPE_DIM:(hh + 1) * NOPE_DIM], kr, gain, cos, sin, 1.0)
        k_ref[:, hh * MLA_PAD:hh * MLA_PAD + NOPE_DIM] = y0.astype(k_ref.dtype)
        k_ref[:, hh * MLA_PAD + NOPE_DIM:(hh + 1) * MLA_PAD] = y1.astype(k_ref.dtype)


def kv_up(c_kv, wk, wv, k_rope, cos, sin, k_gain_pad, *, tm, hpt):
    T = c_kv.shape[0]
    tab = pl.BlockSpec((tm, LANES), lambda i, j: (i, 0))
    return pl.pallas_call(
        functools.partial(_kv_up_kernel, hpt=hpt),
        grid=(T // tm, N_HEADS // hpt),
        in_specs=[
            pl.BlockSpec((tm, KV_LORA), lambda i, j: (i, 0)),
            pl.BlockSpec((KV_LORA, hpt * NOPE_DIM), lambda i, j: (0, j)),
            pl.BlockSpec((KV_LORA, hpt * HEAD_DIM), lambda i, j: (0, j)),
            tab, tab, tab,
            pl.BlockSpec((1, MLA_PAD), lambda i, j: (0, 0)),
        ],
        out_specs=[pl.BlockSpec((tm, hpt * MLA_PAD), lambda i, j: (i, j)),
                   pl.BlockSpec((tm, hpt * HEAD_DIM), lambda i, j: (i, j))],
        out_shape=[jax.ShapeDtypeStruct((T, N_HEADS * MLA_PAD), BF16),
                   jax.ShapeDtypeStruct((T, N_HEADS * HEAD_DIM), BF16)],
        compiler_params=_params(2), name="kv_up",
    )(c_kv, wk, wv, k_rope, cos, sin, k_gain_pad)


def _q_up_kernel(c_ref, w_ref, cos_ref, sin_ref, qg_ref, q_ref, *, hpt):
    acc = jnp.dot(c_ref[...], w_ref[...], preferred_element_type=F32)
    cos, sin, gain = cos_ref[...], sin_ref[...], qg_ref[...]
    for hh in range(hpt):
        lo = hh * MLA_PAD
        y0, y1 = _mla_norm_rope(acc[:, lo:lo + NOPE_DIM], acc[:, lo + NOPE_DIM:lo + MLA_PAD],
                                gain, cos, sin, QK_DIM_B ** -0.5 * LOG2E)
        q_ref[:, lo:lo + NOPE_DIM] = y0.astype(q_ref.dtype)
        q_ref[:, lo + NOPE_DIM:lo + MLA_PAD] = y1.astype(q_ref.dtype)


def q_up(proj_b, w_uq_pad, cos, sin, q_gain_pad, *, tm, hpt):
    T = proj_b.shape[0]
    tab = pl.BlockSpec((tm, LANES), lambda i, j: (i, 0))
    return pl.pallas_call(
        functools.partial(_q_up_kernel, hpt=hpt),
        grid=(T // tm, N_HEADS // hpt),
        in_specs=[
            pl.BlockSpec((tm, Q_LORA), lambda i, j: (i, 0)),
            pl.BlockSpec((Q_LORA, hpt * MLA_PAD), lambda i, j: (0, j)),
            tab, tab,
            pl.BlockSpec((1, MLA_PAD), lambda i, j: (0, 0)),
        ],
        out_specs=pl.BlockSpec((tm, hpt * MLA_PAD), lambda i, j: (i, j)),
        out_shape=jax.ShapeDtypeStruct((T, N_HEADS * MLA_PAD), BF16),
        compiler_params=_params(2), name="q_up",
    )(proj_b, w_uq_pad, cos, sin, q_gain_pad)


def _pad_cols(w, n):
    return jnp.pad(w, ((0, 0), (0, n - w.shape[1])))


def _pad_heads(w, real, padded):
    K = w.shape[0]
    w = w.reshape(K, N_HEADS, real)
    return jnp.pad(w, ((0, 0), (0, 0), (0, padded - real))).reshape(K, N_HEADS * padded)


def kernel(x, mem, positions, a_norm, a_w_in, a_b_forget, a_q_gain, a_k_gain, a_w_out, b_norm, b_w_in, b_q_latent_norm, b_w_uq, b_q_gain, b_w_out, s_norm, s_w_dkv, s_kv_latent_norm, s_w_ukv, s_k_gain, m_norm, m_w_kv, m_q_gain, m_k_gain, f_norm, d_w_gate, d_w_up, d_w_down, e_router, e_w_gate, e_w_up, e_w_down):
    B, S, D = x.shape
    M = mem.shape[1]
    T = B * S
    depth = f_norm.shape[0]
    n_a = a_w_in.shape[0]
    d_a = N_HEADS * HEAD_DIM
    d_memq = N_HEADS_MEM * HEAD_DIM

    tm = min(512, T)
    tq = min(512, S)
    tk = min(512, S)
    tq_mem = min(1024, S)
    tm_e = min(1024, T)
    tc = min(256, T)
    tf_dense = 512
    tf_moe = 512

    h = x.reshape(T, D)
    mkv = mem_kv(mem.reshape(B * M, D), m_norm, m_w_kv.astype(BF16), m_k_gain)
    q_scale = HEAD_DIM ** -0.5
    cos = sin = shared_k = shared_v = None

    for l in range(depth):
        memq_gain = jnp.tile(m_q_gain[l] * q_scale, N_HEADS_MEM)
        if l < n_a:
            w_in = a_w_in[l]
            w_main = jnp.concatenate([w_in[:, :4 * d_a], w_in[:, 4 * d_a + N_HEADS:]], axis=1).astype(BF16)
            w_f = _pad_cols(w_in[:, 4 * d_a:4 * d_a + N_HEADS], LANES).astype(BF16)
            colgain = jnp.concatenate([
                jnp.tile(a_q_gain[l] * (q_scale * LOG2E), N_HEADS), jnp.tile(a_k_gain[l], N_HEADS),
                jnp.ones((2 * d_a,), F32), memq_gain])
            tn = 512
            nt = d_a // tn
            modes = ((0, 2 * nt, HEAD_DIM), (2 * nt, 4 * nt, 0), (4 * nt, 4 * nt + d_memq // tn, HEAD_DIM))
            proj, f_logit = norm_proj(h, a_norm[l], w_main, colgain, modes, tm=tm, tn=tn, wf=w_f)
            f_t = f_logit[:, :N_HEADS].reshape(B, S, N_HEADS).transpose(0, 2, 1).reshape(B * N_HEADS, S)
            bias = jnp.tile(a_b_forget[l], B).reshape(B * N_HEADS, 1)
            f_cum = forget_scan(f_t, bias).reshape(B * N_HEADS, 1, S)
            mix = causal_attention(proj, 0, proj, N_HEADS, proj, 2 * N_HEADS, B=B, S=S,
                                   dk=HEAD_DIM, dv=HEAD_DIM, tq=tq, tk=tk,
                                   f_cum=f_cum, g_arr=proj, g_col0=3 * N_HEADS)
            mo = mem_attention(proj, 4 * N_HEADS, mkv, l, B=B, S=S, M=M, tq=tq_mem)
            w_out = a_w_out[l]
        else:
            jb = l - n_a
            if shared_k is None:
                cos, sin = rope_tables(positions.reshape(T, 1).astype(F32), tm=tm)
                c_kv, k_rope = latent_kv(h, s_norm, s_w_dkv[:, :KV_LORA].astype(BF16),
                                         _pad_cols(s_w_dkv[:, KV_LORA:], LANES).astype(BF16),
                                         s_kv_latent_norm, tm=tm)
                w_ukv = s_w_ukv.reshape(KV_LORA, N_HEADS, NOPE_DIM + HEAD_DIM)
                wk = w_ukv[:, :, :NOPE_DIM].reshape(KV_LORA, N_HEADS * NOPE_DIM).astype(BF16)
                wv = w_ukv[:, :, NOPE_DIM:].reshape(KV_LORA, N_HEADS * HEAD_DIM).astype(BF16)
                k_gain_pad = jnp.pad(s_k_gain, (0, MLA_PAD - QK_DIM_B)).reshape(1, MLA_PAD)
                shared_k, shared_v = kv_up(c_kv, wk, wv, k_rope, cos, sin, k_gain_pad, tm=tm, hpt=4)
            colgain = jnp.concatenate([b_q_latent_norm[jb], memq_gain])
            modes = ((0, 1, Q_LORA), (1, 2, HEAD_DIM))
            proj = norm_proj(h, b_norm[jb], b_w_in[jb].astype(BF16), colgain, modes, tm=tm, tn=512)
            q_gain_pad = jnp.pad(b_q_gain[jb], (0, MLA_PAD - QK_DIM_B)).reshape(1, MLA_PAD)
            q = q_up(proj, _pad_heads(b_w_uq[jb], QK_DIM_B, MLA_PAD).astype(BF16), cos, sin,
                     q_gain_pad, tm=tm, hpt=2)
            mix = causal_attention(q, 0, shared_k, 0, shared_v, 0, B=B, S=S,
                                   dk=MLA_PAD, dv=HEAD_DIM, tq=tq, tk=tk)
            mo = mem_attention(proj, Q_LORA // HEAD_DIM, mkv, l, B=B, S=S, M=M, tq=tq_mem)
            w_out = b_w_out[jb]

        d_mix = mix.shape[1]
        wa, wm = w_out[:d_mix].astype(BF16), w_out[d_mix:].astype(BF16)
        i = l // 2
        if l % 2 == 0:
            h = out_proj(mix, mo, wa, wm, h, tm=tm)
            h = dense_ffn(h, f_norm[l], d_w_gate[i].astype(BF16), d_w_up[i].astype(BF16),
                          d_w_down[i].astype(BF16), tm=tm, tf=tf_dense)
        else:
            h, ids, wts = out_proj(mix, mo, wa, wm, h, tm=tm, f_gain=f_norm[l],
                                   w_router=_pad_cols(e_router[i], LANES))
            src, pos1, pos2, te, n_valid, tile_rows = _route_tables(ids, T, tm_e)
            x_sorted = gather_norm_rows(h, src, n_valid, f_norm[l], tm=tm_e)
            y = moe_ffn(x_sorted, te, n_valid, tile_rows, e_w_gate[i], e_w_up[i], e_w_down[i],
                        tm=tm_e, tf=tf_moe, sub=min(512, tm_e))
            h = moe_combine(h, wts, y, pos1, pos2, tc=tc)
    return h.reshape(B, S, D)
```

```python
import functools
import math

import jax
import jax.numpy as jnp
from jax import lax
from jax.experimental import pallas as pl
from jax.experimental.pallas import tpu as pltpu

F32 = jnp.float32
BF16 = jnp.bfloat16

EPS = 1e-6
HEAD_DIM = 128
N_HEADS = 12
N_HEADS_MEM = 4
NOPE_DIM = 128
ROPE_DIM = 64
ROPE_HALF = ROPE_DIM // 2
QK_DIM_B = NOPE_DIM + ROPE_DIM
MLA_PAD = 256
Q_LORA = 512
KV_LORA = 512
ROPE_THETA = 10000.0
N_EXPERTS = 8
LANES = 128
LOG2E = math.log2(math.e)
ATTN_HEADS_PER_STEP = 2
VMEM_LIMIT = 56 * 1024 * 1024
MOE_VMEM_LIMIT = 60 * 1024 * 1024


def _params(n_grid, vmem=VMEM_LIMIT):
    return pltpu.CompilerParams(dimension_semantics=("arbitrary",) * n_grid, vmem_limit_bytes=vmem)


def _rms(x, gain):
    ms = jnp.mean(x * x, axis=-1, keepdims=True)
    return x * lax.rsqrt(ms + EPS) * gain


def _norm_proj_kernel(*refs, modes, tn, with_f):
    if with_f:
        h_ref, ng_ref, w_ref, cg_ref, wf_ref, o_ref, f_ref, hn_ref = refs
    else:
        h_ref, ng_ref, w_ref, cg_ref, o_ref, hn_ref = refs
    j = pl.program_id(1)

    @pl.when(j == 0)
    def _():
        hn = _rms(h_ref[...], ng_ref[...]).astype(BF16)
        hn_ref[...] = hn
        if with_f:
            f_ref[...] = jnp.dot(hn, wf_ref[...], preferred_element_type=F32)

    acc = jnp.dot(hn_ref[...], w_ref[...], preferred_element_type=F32)
    for lo, hi, gs in modes:
        @pl.when((j >= lo) & (j < hi))
        def _(gs=gs):
            if gs == 0:
                o_ref[...] = acc.astype(o_ref.dtype)
            else:
                for g in range(tn // gs):
                    sl = slice(g * gs, (g + 1) * gs)
                    o_ref[:, sl] = _rms(acc[:, sl], cg_ref[:, sl]).astype(o_ref.dtype)


def norm_proj(h, ngain, w, colgain, modes, *, tm, tn, wf=None):
    T, D = h.shape
    N = w.shape[1]
    with_f = wf is not None
    in_specs = [
        pl.BlockSpec((tm, D), lambda i, j: (i, 0)),
        pl.BlockSpec((1, D), lambda i, j: (0, 0)),
        pl.BlockSpec((D, tn), lambda i, j: (0, j)),
        pl.BlockSpec((1, tn), lambda i, j: (0, j)),
    ]
    args = [h, ngain.reshape(1, D), w, colgain.reshape(1, N)]
    out_shape = [jax.ShapeDtypeStruct((T, N), BF16)]
    out_specs = [pl.BlockSpec((tm, tn), lambda i, j: (i, j))]
    if with_f:
        in_specs.append(pl.BlockSpec((D, LANES), lambda i, j: (0, 0)))
        args.append(wf)
        out_shape.append(jax.ShapeDtypeStruct((T, LANES), F32))
        out_specs.append(pl.BlockSpec((tm, LANES), lambda i, j: (i, 0)))
    outs = pl.pallas_call(
        functools.partial(_norm_proj_kernel, modes=modes, tn=tn, with_f=with_f),
        grid=(T // tm, N // tn),
        in_specs=in_specs, out_specs=out_specs, out_shape=out_shape,
        scratch_shapes=[pltpu.VMEM((tm, D), BF16)],
        compiler_params=_params(2), name="norm_proj",
    )(*args)
    return outs if with_f else outs[0]


def _forget_scan_kernel(x_ref, b_ref, tri_ref, o_ref, *, n_chunks):
    z = x_ref[...] + b_ref[...]
    log_f = jnp.minimum(z, 0.0) - jnp.log1p(jnp.exp(-jnp.abs(z)))
    tri = tri_ref[...]
    carry = jnp.zeros((z.shape[0], 1), F32)
    for c in range(n_chunks):
        sl = slice(c * LANES, (c + 1) * LANES)
        cs = jnp.dot(log_f[:, sl], tri, precision=lax.Precision.HIGHEST,
                     preferred_element_type=F32) + carry
        o_ref[:, sl] = cs * LOG2E
        carry = cs[:, LANES - 1:LANES]


def forget_scan(f_logit_t, bias_col):
    R, S = f_logit_t.shape
    tri = (lax.broadcasted_iota(jnp.int32, (LANES, LANES), 0)
           <= lax.broadcasted_iota(jnp.int32, (LANES, LANES), 1)).astype(F32)
    return pl.pallas_call(
        functools.partial(_forget_scan_kernel, n_chunks=S // LANES),
        out_shape=jax.ShapeDtypeStruct((R, S), F32), name="forget_scan",
    )(f_logit_t, bias_col, tri)


def _flash_kernel(*refs, tq, tk, dk, dv, hp, has_f, has_g):
    it = iter(refs)
    q_ref, k_ref, v_ref = next(it), next(it), next(it)
    f_ref = next(it) if has_f else None
    g_ref = next(it) if has_g else None
    o_ref, m_ref, l_ref, acc_ref = next(it), next(it), next(it), next(it)
    i = pl.program_id(2)
    m_ref[...] = jnp.full(m_ref.shape, -jnp.inf, F32)
    l_ref[...] = jnp.zeros(l_ref.shape, F32)
    acc_ref[...] = jnp.zeros(acc_ref.shape, F32)
    rep = tk // LANES

    def step(j, masked):
        off = pl.multiple_of(j * tk, tk)
        for hh in range(hp):
            s = lax.dot_general(q_ref[:, hh * dk:(hh + 1) * dk],
                                k_ref[pl.ds(off, tk), hh * dk:(hh + 1) * dk],
                                (((1,), (1,)), ((), ())), preferred_element_type=F32)
            if has_f:
                s = s - f_ref[hh, :, pl.ds(off, tk)]
            if masked:
                row = i * tq + lax.broadcasted_iota(jnp.int32, (tq, tk), 0)
                col = off + lax.broadcasted_iota(jnp.int32, (tq, tk), 1)
                s = jnp.where(col <= row, s, -jnp.inf)
            m_prev = m_ref[hh]
            m_new = jnp.maximum(m_prev, jnp.max(s, axis=-1, keepdims=True))
            alpha = jnp.exp2(m_prev - m_new)
            p = jnp.exp2(s - jnp.concatenate([m_new] * rep, axis=1))
            l_ref[hh] = alpha * l_ref[hh] + jnp.sum(p, axis=-1, keepdims=True)
            acc_ref[hh] = alpha * acc_ref[hh] + jnp.dot(
                p.astype(BF16), v_ref[pl.ds(off, tk), hh * dv:(hh + 1) * dv],
                preferred_element_type=F32)
            m_ref[hh] = m_new

    r = tq // tk

    def full_step(j, c):
        step(j, False)
        return c

    lax.fori_loop(0, i * r, full_step, 0)
    for d in range(r):
        step(i * r + d, True)
    for hh in range(hp):
        o = acc_ref[hh] / l_ref[hh]
        if has_g:
            o = o * jax.nn.sigmoid(g_ref[:, hh * dv:(hh + 1) * dv].astype(F32))
        o_ref[:, hh * dv:(hh + 1) * dv] = o.astype(o_ref.dtype)


def causal_attention(q_arr, q_col0, k_arr, k_col0, v_arr, v_col0, *, B, S, dk, dv, tq, tk,
                     f_cum=None, g_arr=None, g_col0=0):
    hp = ATTN_HEADS_PER_STEP
    assert dv == LANES and tq % tk == 0
    assert q_col0 % hp == 0 and k_col0 % hp == 0 and v_col0 % hp == 0 and g_col0 % hp == 0
    nq = S // tq
    T = B * S
    has_f, has_g = f_cum is not None, g_arr is not None
    in_specs = [
        pl.BlockSpec((tq, hp * dk), lambda b, h, i: (b * nq + i, q_col0 // hp + h)),
        pl.BlockSpec((S, hp * dk), lambda b, h, i: (b, k_col0 // hp + h)),
        pl.BlockSpec((S, hp * dv), lambda b, h, i: (b, v_col0 // hp + h)),
    ]
    args = [q_arr, k_arr, v_arr]
    if has_f:
        in_specs.append(pl.BlockSpec((hp, 1, S), lambda b, h, i: (b * (N_HEADS // hp) + h, 0, 0)))
        args.append(f_cum)
    if has_g:
        in_specs.append(pl.BlockSpec((tq, hp * dv), lambda b, h, i: (b * nq + i, g_col0 // hp + h)))
        args.append(g_arr)
    return pl.pallas_call(
        functools.partial(_flash_kernel, tq=tq, tk=tk, dk=dk, dv=dv, hp=hp, has_f=has_f, has_g=has_g),
        grid=(B, N_HEADS // hp, nq),
        in_specs=in_specs,
        out_specs=pl.BlockSpec((tq, hp * dv), lambda b, h, i: (b * nq + i, h)),
        out_shape=jax.ShapeDtypeStruct((T, N_HEADS * dv), BF16),
        scratch_shapes=[pltpu.VMEM((hp, tq, LANES), F32), pltpu.VMEM((hp, tq, LANES), F32),
                        pltpu.VMEM((hp, tq, dv), F32)],
        compiler_params=_params(3), name="causal_attention",
    )(*args)


def _mem_kv_kernel(mem_ref, ng_ref, w_ref, kg_ref, o_ref, *, n_norm_cols):
    x = _rms(mem_ref[...], ng_ref[...]).astype(BF16)
    acc = jnp.dot(x, w_ref[...], preferred_element_type=F32)
    for g in range(acc.shape[1] // HEAD_DIM):
        sl = slice(g * HEAD_DIM, (g + 1) * HEAD_DIM)
        if g * HEAD_DIM < n_norm_cols:
            o_ref[:, sl] = _rms(acc[:, sl], kg_ref[...]).astype(o_ref.dtype)
        else:
            o_ref[:, sl] = acc[:, sl].astype(o_ref.dtype)


def mem_kv(mem2d, m_norm, m_w_kv, m_k_gain):
    L, D, N = m_w_kv.shape
    R = mem2d.shape[0]
    return pl.pallas_call(
        functools.partial(_mem_kv_kernel, n_norm_cols=N // 2),
        grid=(L,),
        in_specs=[
            pl.BlockSpec((R, D), lambda l: (0, 0)),
            pl.BlockSpec((None, 1, D), lambda l: (l, 0, 0)),
            pl.BlockSpec((None, D, N), lambda l: (l, 0, 0)),
            pl.BlockSpec((None, 1, HEAD_DIM), lambda l: (l, 0, 0)),
        ],
        out_specs=pl.BlockSpec((None, R, N), lambda l: (l, 0, 0)),
        out_shape=jax.ShapeDtypeStruct((L, R, N), BF16),
        compiler_params=_params(1), name="mem_kv",
    )(mem2d, m_norm.reshape(L, 1, D), m_w_kv, m_k_gain.reshape(L, 1, HEAD_DIM))


def _mem_attn_kernel(q_ref, k_ref, v_ref, o_ref):
    s = lax.dot_general(q_ref[...], k_ref[...], (((1,), (1,)), ((), ())),
                        preferred_element_type=F32)
    m = jnp.max(s, axis=-1, keepdims=True)
    p = jnp.exp(s - m)
    l = jnp.sum(p, axis=-1, keepdims=True)
    o = jnp.dot(p.astype(BF16), v_ref[...], preferred_element_type=F32) / l
    o_ref[...] = o.astype(o_ref.dtype)


def mem_attention(q_arr, q_col0, mkv, layer, *, B, S, M, tq):
    nq = S // tq
    T = B * S
    return pl.pallas_call(
        _mem_attn_kernel,
        grid=(B, N_HEADS_MEM, nq),
        in_specs=[
            pl.BlockSpec((tq, HEAD_DIM), lambda b, h, i: (b * nq + i, q_col0 + h)),
            pl.BlockSpec((None, M, HEAD_DIM), lambda b, h, i: (layer, b, h)),
            pl.BlockSpec((None, M, HEAD_DIM), lambda b, h, i: (layer, b, N_HEADS_MEM + h)),
        ],
        out_specs=pl.BlockSpec((tq, HEAD_DIM), lambda b, h, i: (b * nq + i, h)),
        out_shape=jax.ShapeDtypeStruct((T, N_HEADS_MEM * HEAD_DIM), BF16),
        compiler_params=_params(3), name="mem_attention",
    )(q_arr, mkv, mkv)


def _out_proj_kernel(*refs, with_router):
    if with_router:
        xa_ref, xm_ref, wa_ref, wm_ref, h_ref, fg_ref, wr_ref, o_ref, ids_ref, wts_ref = refs
    else:
        xa_ref, xm_ref, wa_ref, wm_ref, h_ref, o_ref = refs
    acc = jnp.dot(xa_ref[...], wa_ref[...], preferred_element_type=F32)
    acc = acc + jnp.dot(xm_ref[...], wm_ref[...], preferred_element_type=F32)
    h_new = h_ref[...] + acc
    o_ref[...] = h_new
    if with_router:
        hn = _rms(h_new, fg_ref[...])
        hn_hi = hn.astype(BF16)
        hn_lo = (hn - hn_hi.astype(F32)).astype(BF16)
        wr = wr_ref[...]
        wr_hi = wr.astype(BF16)
        wr_lo = (wr - wr_hi.astype(F32)).astype(BF16)
        logits = (jnp.dot(hn_hi, wr_hi, preferred_element_type=F32)
                  + jnp.dot(hn_hi, wr_lo, preferred_element_type=F32)
                  + jnp.dot(hn_lo, wr_hi, preferred_element_type=F32))
        lane = lax.broadcasted_iota(jnp.int32, logits.shape, 1)
        lane_f = lane.astype(F32)
        logits = jnp.where(lane < N_EXPERTS, logits, -jnp.inf)
        l1 = jnp.max(logits, axis=-1, keepdims=True)
        i1 = jnp.min(jnp.where(logits == l1, lane_f, float(LANES)), axis=-1, keepdims=True)
        rest = jnp.where(lane_f == i1, -jnp.inf, logits)
        l2 = jnp.max(rest, axis=-1, keepdims=True)
        i2 = jnp.min(jnp.where(rest == l2, lane_f, float(LANES)), axis=-1, keepdims=True)
        e = jnp.exp(l2 - l1)
        w1 = 1.0 / (1.0 + e)
        w2 = e / (1.0 + e)
        ids_ref[...] = jnp.where(lane == 0, i1, jnp.where(lane == 1, i2, 0.0))
        wts_ref[...] = jnp.where(lane == 0, w1, jnp.where(lane == 1, w2, 0.0))


def out_proj(xa, xm, wa, wm, h, *, tm, f_gain=None, w_router=None):
    T, D = h.shape
    Ka, Km = xa.shape[1], xm.shape[1]
    with_router = w_router is not None
    in_specs = [
        pl.BlockSpec((tm, Ka), lambda i: (i, 0)),
        pl.BlockSpec((tm, Km), lambda i: (i, 0)),
        pl.BlockSpec((Ka, D), lambda i: (0, 0)),
        pl.BlockSpec((Km, D), lambda i: (0, 0)),
        pl.BlockSpec((tm, D), lambda i: (i, 0)),
    ]
    args = [xa, xm, wa, wm, h]
    out_shape = [jax.ShapeDtypeStruct((T, D), F32)]
    out_specs = [pl.BlockSpec((tm, D), lambda i: (i, 0))]
    if with_router:
        in_specs += [pl.BlockSpec((1, D), lambda i: (0, 0)),
                     pl.BlockSpec((D, LANES), lambda i: (0, 0))]
        args += [f_gain.reshape(1, D), w_router]
        out_shape += [jax.ShapeDtypeStruct((T, LANES), F32)] * 2
        out_specs += [pl.BlockSpec((tm, LANES), lambda i: (i, 0))] * 2
    outs = pl.pallas_call(
        functools.partial(_out_proj_kernel, with_router=with_router),
        grid=(T // tm,),
        in_specs=in_specs, out_specs=out_specs, out_shape=out_shape,
        compiler_params=_params(1), name="out_proj",
    )(*args)
    return outs if with_router else outs[0]


def _swiglu_step(hn, wg_ref, wu_ref, wd_ref):
    g = jnp.dot(hn, wg_ref[...], preferred_element_type=F32)
    u = jnp.dot(hn, wu_ref[...], preferred_element_type=F32)
    a = (g * jax.nn.sigmoid(g)) * u
    return jnp.dot(a.astype(BF16), wd_ref[...], preferred_element_type=F32)


def _dense_ffn_kernel(h_ref, fg_ref, wg_ref, wu_ref, wd_ref, o_ref, hn_ref, acc_ref):
    f = pl.program_id(1)

    @pl.when(f == 0)
    def _():
        hn_ref[...] = _rms(h_ref[...], fg_ref[...]).astype(BF16)
        acc_ref[...] = jnp.zeros(acc_ref.shape, F32)

    acc_ref[...] += _swiglu_step(hn_ref[...], wg_ref, wu_ref, wd_ref)

    @pl.when(f == pl.num_programs(1) - 1)
    def _():
        o_ref[...] = h_ref[...] + acc_ref[...]


def dense_ffn(h, f_gain, wg, wu, wd, *, tm, tf):
    T, D = h.shape
    FF = wg.shape[1]
    return pl.pallas_call(
        _dense_ffn_kernel,
        grid=(T // tm, FF // tf),
        in_specs=[
            pl.BlockSpec((tm, D), lambda i, f: (i, 0)),
            pl.BlockSpec((1, D), lambda i, f: (0, 0)),
            pl.BlockSpec((D, tf), lambda i, f: (0, f)),
            pl.BlockSpec((D, tf), lambda i, f: (0, f)),
            pl.BlockSpec((tf, D), lambda i, f: (f, 0)),
        ],
        out_specs=pl.BlockSpec((tm, D), lambda i, f: (i, 0)),
        out_shape=jax.ShapeDtypeStruct((T, D), F32),
        scratch_shapes=[pltpu.VMEM((tm, D), BF16), pltpu.VMEM((tm, D), F32)],
        compiler_params=_params(2), name="dense_ffn",
    )(h, f_gain.reshape(1, D), wg, wu, wd)


def _moe_ffn_kernel(te_ref, nv_ref, rows_ref, x_ref, wg_ref, wu_ref, wd_ref, y_ref, *, sub):
    i, f = pl.program_id(0), pl.program_id(1)
    tm = x_ref.shape[0]
    rows = rows_ref[i]

    def accumulate(n_rows):
        x = x_ref[0:n_rows, :]
        g = jnp.dot(x, wg_ref[...].astype(BF16), preferred_element_type=F32)
        u = jnp.dot(x, wu_ref[...].astype(BF16), preferred_element_type=F32)
        a = (g * jax.nn.sigmoid(g)) * u
        d = jnp.dot(a.astype(BF16), wd_ref[...].astype(BF16), preferred_element_type=F32)

        @pl.when(f == 0)
        def _():
            y_ref[0:n_rows, :] = d

        @pl.when(f > 0)
        def _():
            y_ref[0:n_rows, :] += d

    @pl.when(rows > sub)
    def _():
        accumulate(tm)

    @pl.when((rows > 0) & (rows <= sub))
    def _():
        accumulate(sub)

    @pl.when((rows <= sub) & (f == 0))
    def _():
        lo = jnp.where(rows > 0, sub, 0)
        for sb in range(tm // sub):
            @pl.when(sb * sub >= lo)
            def _(sb=sb):
                y_ref[sb * sub:(sb + 1) * sub, :] = jnp.zeros((sub, y_ref.shape[1]), y_ref.dtype)


def moe_ffn(x_sorted, tile_expert, n_valid, tile_rows, wg, wu, wd, layer, *, tm, tf, sub):
    P, D = x_sorted.shape
    FF = wg.shape[3]
    nf = FF // tf

    def row_map(i, f, te, nv, rows):
        return (jnp.minimum(i, nv[0] - 1), 0)

    def f_idx(i, f, nv):
        return jnp.where(i < nv[0], f, nf - 1)

    grid_spec = pltpu.PrefetchScalarGridSpec(
        num_scalar_prefetch=3,
        grid=(P // tm, nf),
        in_specs=[
            pl.BlockSpec((tm, D), row_map),
            pl.BlockSpec((None, None, D, tf),
                         lambda i, f, te, nv, rows: (layer, te[i], 0, f_idx(i, f, nv))),
            pl.BlockSpec((None, None, D, tf),
                         lambda i, f, te, nv, rows: (layer, te[i], 0, f_idx(i, f, nv))),
            pl.BlockSpec((None, None, tf, D),
                         lambda i, f, te, nv, rows: (layer, te[i], f_idx(i, f, nv), 0)),
        ],
        out_specs=pl.BlockSpec((tm, D), lambda i, f, te, nv, rows: (i, 0),
                               pipeline_mode=pl.Buffered(1)),
    )
    return pl.pallas_call(
        functools.partial(_moe_ffn_kernel, sub=sub), grid_spec=grid_spec,
        out_shape=jax.ShapeDtypeStruct((P, D), F32),
        compiler_params=_params(2, MOE_VMEM_LIMIT), name="moe_ffn",
    )(tile_expert, n_valid, tile_rows, x_sorted, wg, wu, wd)


def _gather_norm_kernel(src_ref, nv_ref, h_any, fg_ref, o_ref, buf, sem, *, tm):
    i = pl.program_id(0)
    n_valid = nv_ref[0]

    def start_tile(tile, slot):
        def issue(r, c):
            tok = src_ref[tile * tm + r]
            pltpu.make_async_copy(h_any.at[pl.ds(tok, 1), :], buf.at[slot, pl.ds(r, 1), :],
                                  sem.at[slot]).start()
            return c

        lax.fori_loop(0, tm, issue, 0)

    @pl.when(i == 0)
    def _():
        start_tile(0, 0)

    @pl.when(i + 1 < n_valid)
    def _():
        start_tile(i + 1, (i + 1) % 2)

    @pl.when(i < n_valid)
    def _():
        slot = i % 2
        pltpu.make_async_copy(h_any.at[pl.ds(0, tm), :], buf.at[slot], sem.at[slot]).wait()
        o_ref[...] = _rms(buf[slot], fg_ref[...]).astype(o_ref.dtype)

    @pl.when(i >= n_valid)
    def _():
        o_ref[...] = jnp.zeros(o_ref.shape, o_ref.dtype)


def gather_norm_rows(h, src_tok, n_valid, f_gain, *, tm):
    T, D = h.shape
    P = src_tok.shape[0]
    grid_spec = pltpu.PrefetchScalarGridSpec(
        num_scalar_prefetch=2,
        grid=(P // tm,),
        in_specs=[pl.BlockSpec(memory_space=pl.ANY),
                  pl.BlockSpec((1, D), lambda i, src, nv: (0, 0))],
        out_specs=pl.BlockSpec((tm, D), lambda i, src, nv: (i, 0)),
        scratch_shapes=[pltpu.VMEM((2, tm, D), F32), pltpu.SemaphoreType.DMA((2,))],
    )
    return pl.pallas_call(
        functools.partial(_gather_norm_kernel, tm=tm), grid_spec=grid_spec,
        out_shape=jax.ShapeDtypeStruct((P, D), BF16),
        compiler_params=_params(1), name="gather_norm_rows",
    )(src_tok, n_valid, h, f_gain.reshape(1, D))


def _combine_kernel(p1_ref, p2_ref, h_ref, w_ref, y_any, o_ref, buf, sem, *, tc):
    i = pl.program_id(0)

    def start_tile(tile, slot):
        def issue(r, c):
            pltpu.make_async_copy(y_any.at[pl.ds(p1_ref[tile * tc + r], 1), :],
                                  buf.at[slot, 0, pl.ds(r, 1), :], sem.at[slot]).start()
            pltpu.make_async_copy(y_any.at[pl.ds(p2_ref[tile * tc + r], 1), :],
                                  buf.at[slot, 1, pl.ds(r, 1), :], sem.at[slot]).start()
            return c

        lax.fori_loop(0, tc, issue, 0)

    @pl.when(i == 0)
    def _():
        start_tile(0, 0)

    @pl.when(i + 1 < pl.num_programs(0))
    def _():
        start_tile(i + 1, (i + 1) % 2)

    slot = i % 2
    for k in range(2):
        pltpu.make_async_copy(y_any.at[pl.ds(0, tc), :], buf.at[slot, k], sem.at[slot]).wait()
    w = w_ref[...]
    o_ref[...] = h_ref[...] + w[:, 0:1] * buf[slot, 0] + w[:, 1:2] * buf[slot, 1]


def moe_combine(h, wts, y, pos1, pos2, *, tc):
    T, D = h.shape
    grid_spec = pltpu.PrefetchScalarGridSpec(
        num_scalar_prefetch=2,
        grid=(T // tc,),
        in_specs=[
            pl.BlockSpec((tc, D), lambda i, p1, p2: (i, 0)),
            pl.BlockSpec((tc, LANES), lambda i, p1, p2: (i, 0)),
            pl.BlockSpec(memory_space=pl.ANY),
        ],
        out_specs=pl.BlockSpec((tc, D), lambda i, p1, p2: (i, 0)),
        scratch_shapes=[pltpu.VMEM((2, 2, tc, D), F32), pltpu.SemaphoreType.DMA((2,))],
    )
    return pl.pallas_call(
        functools.partial(_combine_kernel, tc=tc), grid_spec=grid_spec,
        out_shape=jax.ShapeDtypeStruct((T, D), F32),
        compiler_params=_params(1), name="moe_combine",
    )(pos1, pos2, h, wts, y)


def _route_tables(ids, T, tm):
    e_flat = jnp.concatenate([ids[:, 0], ids[:, 1]]).astype(jnp.int32)
    onehot = (e_flat[:, None] == jnp.arange(N_EXPERTS, dtype=jnp.int32)[None, :]).astype(jnp.int32)
    csum = jnp.cumsum(onehot, axis=0)
    counts = csum[-1]
    rank = jnp.sum((csum - onehot) * onehot, axis=1)
    padded = ((counts + tm - 1) // tm) * tm
    ends = jnp.cumsum(padded)
    starts = ends - padded
    pos = jnp.sum(onehot * starts[None, :], axis=1) + rank
    n_tiles = (2 * T) // tm + N_EXPERTS
    n_valid = ends[-1] // tm
    tile_start = jnp.arange(n_tiles, dtype=jnp.int32) * tm
    te = jnp.sum((tile_start[:, None] >= ends[None, :]).astype(jnp.int32), axis=1)
    last_e = jnp.max(jnp.where(counts > 0, jnp.arange(N_EXPERTS, dtype=jnp.int32), 0))
    te = jnp.minimum(te, last_e)
    tile_rows = jnp.clip(counts[te] - (tile_start - starts[te]), 0, tm)
    tile_rows = jnp.where(tile_start < ends[-1], tile_rows, 0).astype(jnp.int32)
    tok = jnp.arange(T, dtype=jnp.int32)
    src = jnp.zeros((n_tiles * tm,), jnp.int32).at[pos].set(jnp.concatenate([tok, tok]))
    return src, pos[:T], pos[T:], te, n_valid.reshape(1).astype(jnp.int32), tile_rows


def _rope_table_kernel(pos_ref, freq_ref, sign_ref, cos_ref, sin_ref):
    ang = pos_ref[...] * freq_ref[...]
    valid = sign_ref[...] != 0.0
    cos_ref[...] = jnp.where(valid, jnp.cos(ang), 0.0)
    sin_ref[...] = jnp.sin(ang) * sign_ref[...]


def rope_tables(pos_col, *, tm):
    T = pos_col.shape[0]
    inv_freq = jnp.exp(-math.log(ROPE_THETA) * jnp.arange(ROPE_HALF, dtype=F32) / ROPE_HALF)
    zeros = jnp.zeros((LANES - ROPE_DIM,), F32)
    freq = jnp.concatenate([inv_freq, inv_freq, zeros]).reshape(1, LANES)
    sign = jnp.concatenate([-jnp.ones((ROPE_HALF,), F32), jnp.ones((ROPE_HALF,), F32),
                            zeros]).reshape(1, LANES)
    row = pl.BlockSpec((1, LANES), lambda i: (0, 0))
    tab = pl.BlockSpec((tm, LANES), lambda i: (i, 0))
    return pl.pallas_call(
        _rope_table_kernel, grid=(T // tm,),
        in_specs=[pl.BlockSpec((tm, 1), lambda i: (i, 0)), row, row],
        out_specs=[tab, tab],
        out_shape=[jax.ShapeDtypeStruct((T, LANES), F32)] * 2,
        compiler_params=_params(1), name="rope_tables",
    )(pos_col, freq, sign)


def _mla_norm_rope(a0, a1, gain, cos, sin, scale):
    ss = jnp.sum(a0 * a0, axis=-1, keepdims=True) + jnp.sum(a1 * a1, axis=-1, keepdims=True)
    r = lax.rsqrt(ss * (1.0 / QK_DIM_B) + EPS)
    y0 = a0 * r * gain[:, :NOPE_DIM]
    y1 = a1 * r * gain[:, NOPE_DIM:]
    lane = lax.broadcasted_iota(jnp.int32, y1.shape, 1)
    partner = jnp.where(lane < ROPE_HALF, pltpu.roll(y1, LANES - ROPE_HALF, 1),
                        pltpu.roll(y1, ROPE_HALF, 1))
    y1 = y1 * cos + partner * sin
    return y0 * scale, y1 * scale


def _latent_kv_kernel(h_ref, ng_ref, wc_ref, wr_ref, cg_ref, c_ref, kr_ref):
    hn = _rms(h_ref[...], ng_ref[...]).astype(BF16)
    c = jnp.dot(hn, wc_ref[...], preferred_element_type=F32)
    c_ref[...] = _rms(c, cg_ref[...]).astype(c_ref.dtype)
    kr_ref[...] = jnp.dot(hn, wr_ref[...], preferred_element_type=F32)


def latent_kv(h, s_norm, wc, wr, c_gain, *, tm):
    T, D = h.shape
    return pl.pallas_call(
        _latent_kv_kernel, grid=(T // tm,),
        in_specs=[
            pl.BlockSpec((tm, D), lambda i: (i, 0)),
            pl.BlockSpec((1, D), lambda i: (0, 0)),
            pl.BlockSpec((D, KV_LORA), lambda i: (0, 0)),
            pl.BlockSpec((D, LANES), lambda i: (0, 0)),
            pl.BlockSpec((1, KV_LORA), lambda i: (0, 0)),
        ],
        out_specs=[pl.BlockSpec((tm, KV_LORA), lambda i: (i, 0)),
                   pl.BlockSpec((tm, LANES), lambda i: (i, 0))],
        out_shape=[jax.ShapeDtypeStruct((T, KV_LORA), BF16),
                   jax.ShapeDtypeStruct((T, LANES), F32)],
        compiler_params=_params(1), name="latent_kv",
    )(h, s_norm.reshape(1, D), wc, wr, c_gain.reshape(1, KV_LORA))


def _kv_up_kernel(c_ref, wk_ref, wv_ref, kr_ref, cos_ref, sin_ref, kg_ref, k_ref, v_ref, *, hpt):
    c = c_ref[...]
    v_ref[...] = jnp.dot(c, wv_ref[...], preferred_element_type=F32).astype(v_ref.dtype)
    kn = jnp.dot(c, wk_ref[...], preferred_element_type=F32)
    kr, cos, sin, gain = kr_ref[...], cos_ref[...], sin_ref[...], kg_ref[...]
    for hh in range(hpt):
        y0, y1 = _mla_norm_rope(kn[:, hh * NOPE_DIM:(hh + 1) * NOPE_DIM], kr, gain, cos, sin, 1.0)
        k_ref[:, hh * MLA_PAD:hh * MLA_PAD + NOPE_DIM] = y0.astype(k_ref.dtype)
        k_ref[:, hh * MLA_PAD + NOPE_DIM:(hh + 1) * MLA_PAD] = y1.astype(k_ref.dtype)


def kv_up(c_kv, wk, wv, k_rope, cos, sin, k_gain_pad, *, tm, hpt):
    T = c_kv.shape[0]
    tab = pl.BlockSpec((tm, LANES), lambda i, j: (i, 0))
    return pl.pallas_call(
        functools.partial(_kv_up_kernel, hpt=hpt),
        grid=(T // tm, N_HEADS // hpt),
        in_specs=[
            pl.BlockSpec((tm, KV_LORA), lambda i, j: (i, 0)),
            pl.BlockSpec((KV_LORA, hpt * NOPE_DIM), lambda i, j: (0, j)),
            pl.BlockSpec((KV_LORA, hpt * HEAD_DIM), lambda i, j: (0, j)),
            tab, tab, tab,
            pl.BlockSpec((1, MLA_PAD), lambda i, j: (0, 0)),
        ],
        out_specs=[pl.BlockSpec((tm, hpt * MLA_PAD), lambda i, j: (i, j)),
                   pl.BlockSpec((tm, hpt * HEAD_DIM), lambda i, j: (i, j))],
        out_shape=[jax.ShapeDtypeStruct((T, N_HEADS * MLA_PAD), BF16),
                   jax.ShapeDtypeStruct((T, N_HEADS * HEAD_DIM), BF16)],
        compiler_params=_params(2), name="kv_up",
    )(c_kv, wk, wv, k_rope, cos, sin, k_gain_pad)


def _q_up_kernel(c_ref, w_ref, cos_ref, sin_ref, qg_ref, q_ref, *, hpt):
    acc = jnp.dot(c_ref[...], w_ref[...], preferred_element_type=F32)
    cos, sin, gain = cos_ref[...], sin_ref[...], qg_ref[...]
    for hh in range(hpt):
        lo = hh * MLA_PAD
        y0, y1 = _mla_norm_rope(acc[:, lo:lo + NOPE_DIM], acc[:, lo + NOPE_DIM:lo + MLA_PAD],
                                gain, cos, sin, QK_DIM_B ** -0.5 * LOG2E)
        q_ref[:, lo:lo + NOPE_DIM] = y0.astype(q_ref.dtype)
        q_ref[:, lo + NOPE_DIM:lo + MLA_PAD] = y1.astype(q_ref.dtype)


def q_up(proj_b, w_uq_pad, cos, sin, q_gain_pad, *, tm, hpt):
    T = proj_b.shape[0]
    tab = pl.BlockSpec((tm, LANES), lambda i, j: (i, 0))
    return pl.pallas_call(
        functools.partial(_q_up_kernel, hpt=hpt),
        grid=(T // tm, N_HEADS // hpt),
        in_specs=[
            pl.BlockSpec((tm, Q_LORA), lambda i, j: (i, 0)),
            pl.BlockSpec((Q_LORA, hpt * MLA_PAD), lambda i, j: (0, j)),
            tab, tab,
            pl.BlockSpec((1, MLA_PAD), lambda i, j: (0, 0)),
        ],
        out_specs=pl.BlockSpec((tm, hpt * MLA_PAD), lambda i, j: (i, j)),
        out_shape=jax.ShapeDtypeStruct((T, N_HEADS * MLA_PAD), BF16),
        compiler_params=_params(2), name="q_up",
    )(proj_b, w_uq_pad, cos, sin, q_gain_pad)


def _pad_cols(w, n):
    return jnp.pad(w, ((0, 0), (0, n - w.shape[1])))


def _pad_heads(w, real, padded):
    K = w.shape[0]
    w = w.reshape(K, N_HEADS, real)
    return jnp.pad(w, ((0, 0), (0, 0), (0, padded - real))).reshape(K, N_HEADS * padded)


def kernel(x, mem, positions, a_norm, a_w_in, a_b_forget, a_q_gain, a_k_gain, a_w_out, b_norm, b_w_in, b_q_latent_norm, b_w_uq, b_q_gain, b_w_out, s_norm, s_w_dkv, s_kv_latent_norm, s_w_ukv, s_k_gain, m_norm, m_w_kv, m_q_gain, m_k_gain, f_norm, d_w_gate, d_w_up, d_w_down, e_router, e_w_gate, e_w_up, e_w_down):
    B, S, D = x.shape
    M = mem.shape[1]
    T = B * S
    depth = f_norm.shape[0]
    n_a = a_w_in.shape[0]
    d_a = N_HEADS * HEAD_DIM
    d_memq = N_HEADS_MEM * HEAD_DIM

    tm = min(512, T)
    tm_big = min(1024, T)
    tq = min(512, S)
    tk = min(512, S)
    tq_mem = min(1024, S)
    tm_e = min(1024, T)
    tc = min(256, T)
    tf_dense = 512
    tf_moe = 512

    h = x.reshape(T, D)
    mkv = mem_kv(mem.reshape(B * M, D), m_norm, m_w_kv.astype(BF16), m_k_gain)
    q_scale = HEAD_DIM ** -0.5
    cos = sin = shared_k = shared_v = None

    for l in range(depth):
        memq_gain = jnp.tile(m_q_gain[l] * q_scale, N_HEADS_MEM)
        if l < n_a:
            w_in = a_w_in[l]
            w_main = jnp.concatenate([w_in[:, :4 * d_a], w_in[:, 4 * d_a + N_HEADS:]], axis=1).astype(BF16)
            w_f = _pad_cols(w_in[:, 4 * d_a:4 * d_a + N_HEADS], LANES).astype(BF16)
            colgain = jnp.concatenate([
                jnp.tile(a_q_gain[l] * (q_scale * LOG2E), N_HEADS), jnp.tile(a_k_gain[l], N_HEADS),
                jnp.ones((2 * d_a,), F32), memq_gain])
            tn = 512
            nt = d_a // tn
            modes = ((0, 2 * nt, HEAD_DIM), (2 * nt, 4 * nt, 0), (4 * nt, 4 * nt + d_memq // tn, HEAD_DIM))
            proj, f_logit = norm_proj(h, a_norm[l], w_main, colgain, modes, tm=tm_big, tn=tn, wf=w_f)
            f_t = f_logit[:, :N_HEADS].reshape(B, S, N_HEADS).transpose(0, 2, 1).reshape(B * N_HEADS, S)
            bias = jnp.tile(a_b_forget[l], B).reshape(B * N_HEADS, 1)
            f_cum = forget_scan(f_t, bias).reshape(B * N_HEADS, 1, S)
            mix = causal_attention(proj, 0, proj, N_HEADS, proj, 2 * N_HEADS, B=B, S=S,
                                   dk=HEAD_DIM, dv=HEAD_DIM, tq=tq, tk=tk,
                                   f_cum=f_cum, g_arr=proj, g_col0=3 * N_HEADS)
            mo = mem_attention(proj, 4 * N_HEADS, mkv, l, B=B, S=S, M=M, tq=tq_mem)
            w_out = a_w_out[l]
        else:
            jb = l - n_a
            if shared_k is None:
                cos, sin = rope_tables(positions.reshape(T, 1).astype(F32), tm=tm)
                c_kv, k_rope = latent_kv(h, s_norm, s_w_dkv[:, :KV_LORA].astype(BF16),
                                         _pad_cols(s_w_dkv[:, KV_LORA:], LANES).astype(BF16),
                                         s_kv_latent_norm, tm=tm)
                w_ukv = s_w_ukv.reshape(KV_LORA, N_HEADS, NOPE_DIM + HEAD_DIM)
                wk = w_ukv[:, :, :NOPE_DIM].reshape(KV_LORA, N_HEADS * NOPE_DIM).astype(BF16)
                wv = w_ukv[:, :, NOPE_DIM:].reshape(KV_LORA, N_HEADS * HEAD_DIM).astype(BF16)
                k_gain_pad = jnp.pad(s_k_gain, (0, MLA_PAD - QK_DIM_B)).reshape(1, MLA_PAD)
                shared_k, shared_v = kv_up(c_kv, wk, wv, k_rope, cos, sin, k_gain_pad, tm=tm_big, hpt=6)
            colgain = jnp.concatenate([b_q_latent_norm[jb], memq_gain])
            modes = ((0, 1, Q_LORA), (1, 2, HEAD_DIM))
            proj = norm_proj(h, b_norm[jb], b_w_in[jb].astype(BF16), colgain, modes, tm=tm_big, tn=512)
            q_gain_pad = jnp.pad(b_q_gain[jb], (0, MLA_PAD - QK_DIM_B)).reshape(1, MLA_PAD)
            q = q_up(proj, _pad_heads(b_w_uq[jb], QK_DIM_B, MLA_PAD).astype(BF16), cos, sin,
                     q_gain_pad, tm=tm_big, hpt=6)
            mix = causal_attention(q, 0, shared_k, 0, shared_v, 0, B=B, S=S,
                                   dk=MLA_PAD, dv=HEAD_DIM, tq=tq, tk=tk)
            mo = mem_attention(proj, Q_LORA // HEAD_DIM, mkv, l, B=B, S=S, M=M, tq=tq_mem)
            w_out = b_w_out[jb]

        d_mix = mix.shape[1]
        wa, wm = w_out[:d_mix].astype(BF16), w_out[d_mix:].astype(BF16)
        i = l // 2
        if l % 2 == 0:
            h = out_proj(mix, mo, wa, wm, h, tm=tm)
            h = dense_ffn(h, f_norm[l], d_w_gate[i].astype(BF16), d_w_up[i].astype(BF16),
                          d_w_down[i].astype(BF16), tm=tm, tf=tf_dense)
        else:
            h, ids, wts = out_proj(mix, mo, wa, wm, h, tm=tm, f_gain=f_norm[l],
                                   w_router=_pad_cols(e_router[i], LANES))
            src, pos1, pos2, te, n_valid, tile_rows = _route_tables(ids, T, tm_e)
            x_sorted = gather_norm_rows(h, src, n_valid, f_norm[l], tm=tm_e)
            y = moe_ffn(x_sorted, te, n_valid, tile_rows, e_w_gate, e_w_up, e_w_down, i,
                        tm=tm_e, tf=tf_moe, sub=min(512, tm_e))
            h = moe_combine(h, wts, y, pos1, pos2, tc=tc)
    return h.reshape(B, S, D)
```

```python
import functools
import math

import jax
import jax.numpy as jnp
from jax import lax
from jax.experimental import pallas as pl
from jax.experimental.pallas import tpu as pltpu

F32 = jnp.float32
BF16 = jnp.bfloat16

EPS = 1e-6
HEAD_DIM = 128
N_HEADS = 12
N_HEADS_MEM = 4
NOPE_DIM = 128
ROPE_DIM = 64
ROPE_HALF = ROPE_DIM // 2
QK_DIM_B = NOPE_DIM + ROPE_DIM
MLA_PAD = 256
Q_LORA = 512
KV_LORA = 512
ROPE_THETA = 10000.0
N_EXPERTS = 8
LANES = 128
LOG2E = math.log2(math.e)
ATTN_HEADS_PER_STEP = 2
VMEM_LIMIT = 56 * 1024 * 1024
MOE_VMEM_LIMIT = 60 * 1024 * 1024


def _params(n_grid, vmem=VMEM_LIMIT):
    return pltpu.CompilerParams(dimension_semantics=("arbitrary",) * n_grid, vmem_limit_bytes=vmem)


def _rms(x, gain):
    ms = jnp.mean(x * x, axis=-1, keepdims=True)
    return x * lax.rsqrt(ms + EPS) * gain


def _norm_proj_kernel(*refs, modes, tn, with_tail):
    if with_tail:
        h_ref, ng_ref, w_ref, cg_ref, wt_ref, wf_ref, o_ref, f_ref, hn_ref = refs
    else:
        h_ref, ng_ref, w_ref, cg_ref, o_ref, hn_ref = refs
        wt_ref = None
    j = pl.program_id(1)

    @pl.when(j == 0)
    def _():
        hn = _rms(h_ref[...], ng_ref[...]).astype(BF16)
        hn_ref[...] = hn
        if with_tail:
            f_ref[...] = jnp.dot(hn, wf_ref[...], preferred_element_type=F32)

    for lo, hi, gs, use_tail in modes:
        @pl.when((j >= lo) & (j < hi))
        def _(gs=gs, use_tail=use_tail):
            w = wt_ref[...] if use_tail else w_ref[...]
            acc = jnp.dot(hn_ref[...], w, preferred_element_type=F32)
            if gs == 0:
                o_ref[...] = acc.astype(o_ref.dtype)
            else:
                for g in range(tn // gs):
                    sl = slice(g * gs, (g + 1) * gs)
                    o_ref[:, sl] = _rms(acc[:, sl], cg_ref[:, sl]).astype(o_ref.dtype)


def norm_proj(h, ngain, w, colgain, modes, *, tm, tn, w_tail=None, wf=None):
    T, D = h.shape
    n_main = w.shape[1] // tn
    with_tail = w_tail is not None
    N = w.shape[1] + (tn if with_tail else 0)
    in_specs = [
        pl.BlockSpec((tm, D), lambda i, j: (i, 0)),
        pl.BlockSpec((1, D), lambda i, j: (0, 0)),
        pl.BlockSpec((D, tn), lambda i, j: (0, jnp.minimum(j, n_main - 1))),
        pl.BlockSpec((1, tn), lambda i, j: (0, j)),
    ]
    args = [h, ngain.reshape(1, D), w, colgain.reshape(1, N)]
    out_shape = [jax.ShapeDtypeStruct((T, N), BF16)]
    out_specs = [pl.BlockSpec((tm, tn), lambda i, j: (i, j))]
    if with_tail:
        in_specs += [pl.BlockSpec((D, tn), lambda i, j: (0, 0)),
                     pl.BlockSpec((D, LANES), lambda i, j: (0, 0))]
        args += [w_tail, wf]
        out_shape.append(jax.ShapeDtypeStruct((T, LANES), F32))
        out_specs.append(pl.BlockSpec((tm, LANES), lambda i, j: (i, 0)))
    outs = pl.pallas_call(
        functools.partial(_norm_proj_kernel, modes=modes, tn=tn, with_tail=with_tail),
        grid=(T // tm, N // tn),
        in_specs=in_specs, out_specs=out_specs, out_shape=out_shape,
        scratch_shapes=[pltpu.VMEM((tm, D), BF16)],
        compiler_params=_params(2), name="norm_proj",
    )(*args)
    return outs if with_tail else outs[0]


def _tail_prep_kernel(*refs, n_blocks, n_f, n_tail):
    blocks = [r[...] for r in refs[:n_blocks]]
    shift_ref, wmq_ref, wf_ref = refs[n_blocks:]
    lane = lax.broadcasted_iota(jnp.int32, blocks[0].shape, 1)
    blocks[-1] = jnp.where(lane < n_tail - (n_blocks - 1) * LANES, blocks[-1], 0.0)
    tail = jnp.concatenate(blocks, axis=1).astype(BF16)
    wmq_ref[...] = jnp.dot(tail, shift_ref[...], preferred_element_type=F32).astype(BF16)
    wf_ref[...] = jnp.where(lane < n_f, blocks[0], 0.0).astype(BF16)


def tail_prep(w_in_all, layer, col0, n_f, n_mq):
    _, D, n_cols = w_in_all.shape
    n_tail = n_cols - col0
    assert n_tail == n_f + n_mq and col0 % LANES == 0
    n_blocks = pl.cdiv(n_tail, LANES)
    row = lax.broadcasted_iota(jnp.int32, (n_blocks * LANES, n_mq), 0)
    col = lax.broadcasted_iota(jnp.int32, (n_blocks * LANES, n_mq), 1)
    shift = (row == col + n_f).astype(BF16)
    in_specs = [pl.BlockSpec((None, D, LANES), lambda i, k=k: (layer, 0, col0 // LANES + k))
                for k in range(n_blocks)]
    in_specs.append(pl.BlockSpec(shift.shape, lambda i: (0, 0)))
    return pl.pallas_call(
        functools.partial(_tail_prep_kernel, n_blocks=n_blocks, n_f=n_f, n_tail=n_tail),
        grid=(1,),
        in_specs=in_specs,
        out_specs=[pl.BlockSpec((D, n_mq), lambda i: (0, 0)),
                   pl.BlockSpec((D, LANES), lambda i: (0, 0))],
        out_shape=[jax.ShapeDtypeStruct((D, n_mq), BF16), jax.ShapeDtypeStruct((D, LANES), BF16)],
        compiler_params=_params(1), name="tail_prep",
    )(*([w_in_all] * n_blocks), shift)


def _forget_scan_kernel(x_ref, b_ref, tri_ref, o_ref, *, n_chunks):
    z = x_ref[...] + b_ref[...]
    log_f = jnp.minimum(z, 0.0) - jnp.log1p(jnp.exp(-jnp.abs(z)))
    tri = tri_ref[...]
    carry = jnp.zeros((z.shape[0], 1), F32)
    for c in range(n_chunks):
        sl = slice(c * LANES, (c + 1) * LANES)
        cs = jnp.dot(log_f[:, sl], tri, precision=lax.Precision.HIGHEST,
                     preferred_element_type=F32) + carry
        o_ref[:, sl] = cs * LOG2E
        carry = cs[:, LANES - 1:LANES]


def forget_scan(f_logit_t, bias_col):
    R, S = f_logit_t.shape
    tri = (lax.broadcasted_iota(jnp.int32, (LANES, LANES), 0)
           <= lax.broadcasted_iota(jnp.int32, (LANES, LANES), 1)).astype(F32)
    return pl.pallas_call(
        functools.partial(_forget_scan_kernel, n_chunks=S // LANES),
        out_shape=jax.ShapeDtypeStruct((R, S), F32), name="forget_scan",
    )(f_logit_t, bias_col, tri)


def _flash_kernel(*refs, tq, tk, dk, dv, hp, has_f, has_g):
    it = iter(refs)
    q_ref, k_ref, v_ref = next(it), next(it), next(it)
    f_ref = next(it) if has_f else None
    g_ref = next(it) if has_g else None
    o_ref, m_ref, l_ref, acc_ref = next(it), next(it), next(it), next(it)
    i = pl.program_id(2)
    m_ref[...] = jnp.full(m_ref.shape, -jnp.inf, F32)
    l_ref[...] = jnp.zeros(l_ref.shape, F32)
    acc_ref[...] = jnp.zeros(acc_ref.shape, F32)
    rep = tk // LANES

    def step(j, masked):
        off = pl.multiple_of(j * tk, tk)
        for hh in range(hp):
            s = lax.dot_general(q_ref[:, hh * dk:(hh + 1) * dk],
                                k_ref[pl.ds(off, tk), hh * dk:(hh + 1) * dk],
                                (((1,), (1,)), ((), ())), preferred_element_type=F32)
            if has_f:
                s = s - f_ref[hh, :, pl.ds(off, tk)]
            if masked:
                row = i * tq + lax.broadcasted_iota(jnp.int32, (tq, tk), 0)
                col = off + lax.broadcasted_iota(jnp.int32, (tq, tk), 1)
                s = jnp.where(col <= row, s, -jnp.inf)
            m_prev = m_ref[hh]
            m_new = jnp.maximum(m_prev, jnp.max(s, axis=-1, keepdims=True))
            alpha = jnp.exp2(m_prev - m_new)
            p = jnp.exp2(s - jnp.concatenate([m_new] * rep, axis=1))
            l_ref[hh] = alpha * l_ref[hh] + jnp.sum(p, axis=-1, keepdims=True)
            acc_ref[hh] = alpha * acc_ref[hh] + jnp.dot(
                p.astype(BF16), v_ref[pl.ds(off, tk), hh * dv:(hh + 1) * dv],
                preferred_element_type=F32)
            m_ref[hh] = m_new

    r = tq // tk

    def full_step(j, c):
        step(j, False)
        return c

    lax.fori_loop(0, i * r, full_step, 0)
    for d in range(r):
        step(i * r + d, True)
    for hh in range(hp):
        o = acc_ref[hh] / l_ref[hh]
        if has_g:
            o = o * jax.nn.sigmoid(g_ref[:, hh * dv:(hh + 1) * dv].astype(F32))
        o_ref[:, hh * dv:(hh + 1) * dv] = o.astype(o_ref.dtype)


def causal_attention(q_arr, q_col0, k_arr, k_col0, v_arr, v_col0, *, B, S, dk, dv, tq, tk,
                     f_cum=None, g_arr=None, g_col0=0):
    hp = ATTN_HEADS_PER_STEP
    assert dv == LANES and tq % tk == 0
    assert q_col0 % hp == 0 and k_col0 % hp == 0 and v_col0 % hp == 0 and g_col0 % hp == 0
    nq = S // tq
    T = B * S
    has_f, has_g = f_cum is not None, g_arr is not None
    in_specs = [
        pl.BlockSpec((tq, hp * dk), lambda b, h, i: (b * nq + i, q_col0 // hp + h)),
        pl.BlockSpec((S, hp * dk), lambda b, h, i: (b, k_col0 // hp + h)),
        pl.BlockSpec((S, hp * dv), lambda b, h, i: (b, v_col0 // hp + h)),
    ]
    args = [q_arr, k_arr, v_arr]
    if has_f:
        in_specs.append(pl.BlockSpec((hp, 1, S), lambda b, h, i: (b * (N_HEADS // hp) + h, 0, 0)))
        args.append(f_cum)
    if has_g:
        in_specs.append(pl.BlockSpec((tq, hp * dv), lambda b, h, i: (b * nq + i, g_col0 // hp + h)))
        args.append(g_arr)
    return pl.pallas_call(
        functools.partial(_flash_kernel, tq=tq, tk=tk, dk=dk, dv=dv, hp=hp, has_f=has_f, has_g=has_g),
        grid=(B, N_HEADS // hp, nq),
        in_specs=in_specs,
        out_specs=pl.BlockSpec((tq, hp * dv), lambda b, h, i: (b * nq + i, h)),
        out_shape=jax.ShapeDtypeStruct((T, N_HEADS * dv), BF16),
        scratch_shapes=[pltpu.VMEM((hp, tq, LANES), F32), pltpu.VMEM((hp, tq, LANES), F32),
                        pltpu.VMEM((hp, tq, dv), F32)],
        compiler_params=_params(3), name="causal_attention",
    )(*args)


def _mem_kv_kernel(mem_ref, ng_ref, w_ref, kg_ref, o_ref, *, n_norm_cols):
    x = _rms(mem_ref[...], ng_ref[...]).astype(BF16)
    acc = jnp.dot(x, w_ref[...], preferred_element_type=F32)
    for g in range(acc.shape[1] // HEAD_DIM):
        sl = slice(g * HEAD_DIM, (g + 1) * HEAD_DIM)
        if g * HEAD_DIM < n_norm_cols:
            o_ref[:, sl] = _rms(acc[:, sl], kg_ref[...]).astype(o_ref.dtype)
        else:
            o_ref[:, sl] = acc[:, sl].astype(o_ref.dtype)


def mem_kv(mem2d, m_norm, m_w_kv, m_k_gain):
    L, D, N = m_w_kv.shape
    R = mem2d.shape[0]
    return pl.pallas_call(
        functools.partial(_mem_kv_kernel, n_norm_cols=N // 2),
        grid=(L,),
        in_specs=[
            pl.BlockSpec((R, D), lambda l: (0, 0)),
            pl.BlockSpec((None, 1, D), lambda l: (l, 0, 0)),
            pl.BlockSpec((None, D, N), lambda l: (l, 0, 0)),
            pl.BlockSpec((None, 1, HEAD_DIM), lambda l: (l, 0, 0)),
        ],
        out_specs=pl.BlockSpec((None, R, N), lambda l: (l, 0, 0)),
        out_shape=jax.ShapeDtypeStruct((L, R, N), BF16),
        compiler_params=_params(1), name="mem_kv",
    )(mem2d, m_norm.reshape(L, 1, D), m_w_kv, m_k_gain.reshape(L, 1, HEAD_DIM))


def _mem_attn_kernel(q_ref, k_ref, v_ref, o_ref):
    s = lax.dot_general(q_ref[...], k_ref[...], (((1,), (1,)), ((), ())),
                        preferred_element_type=F32)
    m = jnp.max(s, axis=-1, keepdims=True)
    p = jnp.exp(s - m)
    l = jnp.sum(p, axis=-1, keepdims=True)
    o = jnp.dot(p.astype(BF16), v_ref[...], preferred_element_type=F32) / l
    o_ref[...] = o.astype(o_ref.dtype)


def mem_attention(q_arr, q_col0, mkv, layer, *, B, S, M, tq):
    nq = S // tq
    T = B * S
    return pl.pallas_call(
        _mem_attn_kernel,
        grid=(B, N_HEADS_MEM, nq),
        in_specs=[
            pl.BlockSpec((tq, HEAD_DIM), lambda b, h, i: (b * nq + i, q_col0 + h)),
            pl.BlockSpec((None, M, HEAD_DIM), lambda b, h, i: (layer, b, h)),
            pl.BlockSpec((None, M, HEAD_DIM), lambda b, h, i: (layer, b, N_HEADS_MEM + h)),
        ],
        out_specs=pl.BlockSpec((tq, HEAD_DIM), lambda b, h, i: (b * nq + i, h)),
        out_shape=jax.ShapeDtypeStruct((T, N_HEADS_MEM * HEAD_DIM), BF16),
        compiler_params=_params(3), name="mem_attention",
    )(q_arr, mkv, mkv)


def _out_proj_kernel(*refs, with_router):
    if with_router:
        (xa_ref, xm_ref, wa_ref, wm_ref, h_ref, fg_ref, wr_ref,
         o_ref, ids_ref, wts_ref, rows_ref) = refs
    else:
        xa_ref, xm_ref, wa_ref, wm_ref, h_ref, o_ref = refs
    acc = jnp.dot(xa_ref[...], wa_ref[...], preferred_element_type=F32)
    acc = acc + jnp.dot(xm_ref[...], wm_ref[...], preferred_element_type=F32)
    h_new = h_ref[...] + acc
    o_ref[...] = h_new
    if with_router:
        rows_ref[:, 0, :] = h_new
        hn = _rms(h_new, fg_ref[...])
        hn_hi = hn.astype(BF16)
        hn_lo = (hn - hn_hi.astype(F32)).astype(BF16)
        wr = wr_ref[...]
        wr_hi = wr.astype(BF16)
        wr_lo = (wr - wr_hi.astype(F32)).astype(BF16)
        logits = (jnp.dot(hn_hi, wr_hi, preferred_element_type=F32)
                  + jnp.dot(hn_hi, wr_lo, preferred_element_type=F32)
                  + jnp.dot(hn_lo, wr_hi, preferred_element_type=F32))
        lane = lax.broadcasted_iota(jnp.int32, logits.shape, 1)
        lane_f = lane.astype(F32)
        logits = jnp.where(lane < N_EXPERTS, logits, -jnp.inf)
        l1 = jnp.max(logits, axis=-1, keepdims=True)
        i1 = jnp.min(jnp.where(logits == l1, lane_f, float(LANES)), axis=-1, keepdims=True)
        rest = jnp.where(lane_f == i1, -jnp.inf, logits)
        l2 = jnp.max(rest, axis=-1, keepdims=True)
        i2 = jnp.min(jnp.where(rest == l2, lane_f, float(LANES)), axis=-1, keepdims=True)
        e = jnp.exp(l2 - l1)
        w1 = 1.0 / (1.0 + e)
        w2 = e / (1.0 + e)
        ids_ref[...] = jnp.where(lane == 0, i1, jnp.where(lane == 1, i2, 0.0))
        wts_ref[...] = jnp.where(lane == 0, w1, jnp.where(lane == 1, w2, 0.0))


def out_proj(xa, xm, wa, wm, h, *, tm, f_gain=None, w_router=None):
    T, D = h.shape
    Ka, Km = xa.shape[1], xm.shape[1]
    with_router = w_router is not None
    in_specs = [
        pl.BlockSpec((tm, Ka), lambda i: (i, 0)),
        pl.BlockSpec((tm, Km), lambda i: (i, 0)),
        pl.BlockSpec((Ka, D), lambda i: (0, 0)),
        pl.BlockSpec((Km, D), lambda i: (0, 0)),
        pl.BlockSpec((tm, D), lambda i: (i, 0)),
    ]
    args = [xa, xm, wa, wm, h]
    out_shape = [jax.ShapeDtypeStruct((T, D), F32)]
    out_specs = [pl.BlockSpec((tm, D), lambda i: (i, 0))]
    if with_router:
        in_specs += [pl.BlockSpec((1, D), lambda i: (0, 0)),
                     pl.BlockSpec((D, LANES), lambda i: (0, 0))]
        args += [f_gain.reshape(1, D), w_router]
        out_shape += [jax.ShapeDtypeStruct((T, LANES), F32)] * 2
        out_specs += [pl.BlockSpec((tm, LANES), lambda i: (i, 0))] * 2
        out_shape.append(jax.ShapeDtypeStruct((T, 1, D), F32))
        out_specs.append(pl.BlockSpec((tm, 1, D), lambda i: (i, 0, 0)))
    outs = pl.pallas_call(
        functools.partial(_out_proj_kernel, with_router=with_router),
        grid=(T // tm,),
        in_specs=in_specs, out_specs=out_specs, out_shape=out_shape,
        compiler_params=_params(1), name="out_proj",
    )(*args)
    return outs if with_router else outs[0]


def _swiglu_step(hn, wg_ref, wu_ref, wd_ref):
    g = jnp.dot(hn, wg_ref[...], preferred_element_type=F32)
    u = jnp.dot(hn, wu_ref[...], preferred_element_type=F32)
    a = (g * jax.nn.sigmoid(g)) * u
    return jnp.dot(a.astype(BF16), wd_ref[...], preferred_element_type=F32)


def _dense_ffn_kernel(h_ref, fg_ref, wg_ref, wu_ref, wd_ref, o_ref, hn_ref, acc_ref):
    f = pl.program_id(1)

    @pl.when(f == 0)
    def _():
        hn_ref[...] = _rms(h_ref[...], fg_ref[...]).astype(BF16)
        acc_ref[...] = jnp.zeros(acc_ref.shape, F32)

    acc_ref[...] += _swiglu_step(hn_ref[...], wg_ref, wu_ref, wd_ref)

    @pl.when(f == pl.num_programs(1) - 1)
    def _():
        o_ref[...] = h_ref[...] + acc_ref[...]


def dense_ffn(h, f_gain, wg, wu, wd, *, tm, tf):
    T, D = h.shape
    FF = wg.shape[1]
    return pl.pallas_call(
        _dense_ffn_kernel,
        grid=(T // tm, FF // tf),
        in_specs=[
            pl.BlockSpec((tm, D), lambda i, f: (i, 0)),
            pl.BlockSpec((1, D), lambda i, f: (0, 0)),
            pl.BlockSpec((D, tf), lambda i, f: (0, f)),
            pl.BlockSpec((D, tf), lambda i, f: (0, f)),
            pl.BlockSpec((tf, D), lambda i, f: (f, 0)),
        ],
        out_specs=pl.BlockSpec((tm, D), lambda i, f: (i, 0)),
        out_shape=jax.ShapeDtypeStruct((T, D), F32),
        scratch_shapes=[pltpu.VMEM((tm, D), BF16), pltpu.VMEM((tm, D), F32)],
        compiler_params=_params(2), name="dense_ffn",
    )(h, f_gain.reshape(1, D), wg, wu, wd)


def _moe_ffn_kernel(te_ref, nv_ref, rows_ref, x_ref, wg_ref, wu_ref, wd_ref, y_ref, *, sub):
    i, f = pl.program_id(0), pl.program_id(1)
    tm = x_ref.shape[0]
    rows = rows_ref[i]

    def accumulate(n_rows):
        x = x_ref[0:n_rows, :]
        g = jnp.dot(x, wg_ref[...].astype(BF16), preferred_element_type=F32)
        u = jnp.dot(x, wu_ref[...].astype(BF16), preferred_element_type=F32)
        a = (g * jax.nn.sigmoid(g)) * u
        d = jnp.dot(a.astype(BF16), wd_ref[...].astype(BF16), preferred_element_type=F32)

        @pl.when(f == 0)
        def _():
            y_ref[0:n_rows, :] = d

        @pl.when(f > 0)
        def _():
            y_ref[0:n_rows, :] += d

    @pl.when(rows > sub)
    def _():
        accumulate(tm)

    @pl.when((rows > 0) & (rows <= sub))
    def _():
        accumulate(sub)

    @pl.when((rows <= sub) & (f == 0))
    def _():
        lo = jnp.where(rows > 0, sub, 0)
        for sb in range(tm // sub):
            @pl.when(sb * sub >= lo)
            def _(sb=sb):
                y_ref[sb * sub:(sb + 1) * sub, :] = jnp.zeros((sub, y_ref.shape[1]), y_ref.dtype)


def moe_ffn(x_sorted, tile_expert, n_valid, tile_rows, wg, wu, wd, layer, *, tm, tf, sub):
    P, D = x_sorted.shape
    FF = wg.shape[3]
    nf = FF // tf

    def row_map(i, f, te, nv, rows):
        return (jnp.minimum(i, nv[0] - 1), 0)

    def f_idx(i, f, nv):
        return jnp.where(i < nv[0], f, nf - 1)

    grid_spec = pltpu.PrefetchScalarGridSpec(
        num_scalar_prefetch=3,
        grid=(P // tm, nf),
        in_specs=[
            pl.BlockSpec((tm, D), row_map),
            pl.BlockSpec((None, None, D, tf),
                         lambda i, f, te, nv, rows: (layer, te[i], 0, f_idx(i, f, nv))),
            pl.BlockSpec((None, None, D, tf),
                         lambda i, f, te, nv, rows: (layer, te[i], 0, f_idx(i, f, nv))),
            pl.BlockSpec((None, None, tf, D),
                         lambda i, f, te, nv, rows: (layer, te[i], f_idx(i, f, nv), 0)),
        ],
        out_specs=pl.BlockSpec((tm, D), lambda i, f, te, nv, rows: (i, 0),
                               pipeline_mode=pl.Buffered(1)),
    )
    return pl.pallas_call(
        functools.partial(_moe_ffn_kernel, sub=sub), grid_spec=grid_spec,
        out_shape=jax.ShapeDtypeStruct((P, D), F32),
        compiler_params=_params(2, MOE_VMEM_LIMIT), name="moe_ffn",
    )(tile_expert, n_valid, tile_rows, x_sorted, wg, wu, wd)


def _gather_norm_kernel(src_ref, nv_ref, h_any, fg_ref, o_ref, buf, sem, *, tm):
    i = pl.program_id(0)
    n_valid = nv_ref[0]

    def start_tile(tile, slot):
        def issue(r, c):
            tok = src_ref[tile * tm + r]
            pltpu.make_async_copy(h_any.at[tok], buf.at[slot, pl.ds(r, 1), :],
                                  sem.at[slot]).start()
            return c

        lax.fori_loop(0, tm, issue, 0)

    @pl.when(i == 0)
    def _():
        start_tile(0, 0)

    @pl.when(i + 1 < n_valid)
    def _():
        start_tile(i + 1, (i + 1) % 2)

    @pl.when(i < n_valid)
    def _():
        slot = i % 2
        pltpu.make_async_copy(h_any.at[pl.ds(0, tm), 0], buf.at[slot], sem.at[slot]).wait()
        o_ref[...] = _rms(buf[slot], fg_ref[...]).astype(o_ref.dtype)

    @pl.when(i >= n_valid)
    def _():
        o_ref[...] = jnp.zeros(o_ref.shape, o_ref.dtype)


def gather_norm_rows(h, src_tok, n_valid, f_gain, *, tm):
    T, _, D = h.shape
    P = src_tok.shape[0]
    grid_spec = pltpu.PrefetchScalarGridSpec(
        num_scalar_prefetch=2,
        grid=(P // tm,),
        in_specs=[pl.BlockSpec(memory_space=pl.ANY),
                  pl.BlockSpec((1, D), lambda i, src, nv: (0, 0))],
        out_specs=pl.BlockSpec((tm, D), lambda i, src, nv: (i, 0)),
        scratch_shapes=[pltpu.VMEM((2, tm, D), F32), pltpu.SemaphoreType.DMA((2,))],
    )
    return pl.pallas_call(
        functools.partial(_gather_norm_kernel, tm=tm), grid_spec=grid_spec,
        out_shape=jax.ShapeDtypeStruct((P, D), BF16),
        compiler_params=_params(1), name="gather_norm_rows",
    )(src_tok, n_valid, h, f_gain.reshape(1, D))


def _combine_kernel(p1_ref, p2_ref, h_ref, w_ref, y_any, o_ref, buf, sem, *, tc):
    i = pl.program_id(0)

    def start_tile(tile, slot):
        def issue(r, c):
            pltpu.make_async_copy(y_any.at[pl.ds(p1_ref[tile * tc + r], 1), :],
                                  buf.at[slot, 0, pl.ds(r, 1), :], sem.at[slot]).start()
            pltpu.make_async_copy(y_any.at[pl.ds(p2_ref[tile * tc + r], 1), :],
                                  buf.at[slot, 1, pl.ds(r, 1), :], sem.at[slot]).start()
            return c

        lax.fori_loop(0, tc, issue, 0)

    @pl.when(i == 0)
    def _():
        start_tile(0, 0)

    @pl.when(i + 1 < pl.num_programs(0))
    def _():
        start_tile(i + 1, (i + 1) % 2)

    slot = i % 2
    for k in range(2):
        pltpu.make_async_copy(y_any.at[pl.ds(0, tc), :], buf.at[slot, k], sem.at[slot]).wait()
    w = w_ref[...]
    o_ref[...] = h_ref[...] + w[:, 0:1] * buf[slot, 0] + w[:, 1:2] * buf[slot, 1]


def moe_combine(h, wts, y, pos1, pos2, *, tc):
    T, D = h.shape
    grid_spec = pltpu.PrefetchScalarGridSpec(
        num_scalar_prefetch=2,
        grid=(T // tc,),
        in_specs=[
            pl.BlockSpec((tc, D), lambda i, p1, p2: (i, 0)),
            pl.BlockSpec((tc, LANES), lambda i, p1, p2: (i, 0)),
            pl.BlockSpec(memory_space=pl.ANY),
        ],
        out_specs=pl.BlockSpec((tc, D), lambda i, p1, p2: (i, 0)),
        scratch_shapes=[pltpu.VMEM((2, 2, tc, D), F32), pltpu.SemaphoreType.DMA((2,))],
    )
    return pl.pallas_call(
        functools.partial(_combine_kernel, tc=tc), grid_spec=grid_spec,
        out_shape=jax.ShapeDtypeStruct((T, D), F32),
        compiler_params=_params(1), name="moe_combine",
    )(pos1, pos2, h, wts, y)


def _route_tables(ids, T, tm):
    e_flat = jnp.concatenate([ids[:, 0], ids[:, 1]]).astype(jnp.int32)
    onehot = (e_flat[:, None] == jnp.arange(N_EXPERTS, dtype=jnp.int32)[None, :]).astype(jnp.int32)
    csum = jnp.cumsum(onehot, axis=0)
    counts = csum[-1]
    rank = jnp.sum((csum - onehot) * onehot, axis=1)
    padded = ((counts + tm - 1) // tm) * tm
    ends = jnp.cumsum(padded)
    starts = ends - padded
    pos = jnp.sum(onehot * starts[None, :], axis=1) + rank
    n_tiles = (2 * T) // tm + N_EXPERTS
    n_valid = ends[-1] // tm
    tile_start = jnp.arange(n_tiles, dtype=jnp.int32) * tm
    te = jnp.sum((tile_start[:, None] >= ends[None, :]).astype(jnp.int32), axis=1)
    last_e = jnp.max(jnp.where(counts > 0, jnp.arange(N_EXPERTS, dtype=jnp.int32), 0))
    te = jnp.minimum(te, last_e)
    tile_rows = jnp.clip(counts[te] - (tile_start - starts[te]), 0, tm)
    tile_rows = jnp.where(tile_start < ends[-1], tile_rows, 0).astype(jnp.int32)
    tok = jnp.arange(T, dtype=jnp.int32)
    src = jnp.zeros((n_tiles * tm,), jnp.int32).at[pos].set(jnp.concatenate([tok, tok]))
    return src, pos[:T], pos[T:], te, n_valid.reshape(1).astype(jnp.int32), tile_rows


def _rope_table_kernel(pos_ref, freq_ref, sign_ref, cos_ref, sin_ref):
    ang = pos_ref[...] * freq_ref[...]
    valid = sign_ref[...] != 0.0
    cos_ref[...] = jnp.where(valid, jnp.cos(ang), 0.0)
    sin_ref[...] = jnp.sin(ang) * sign_ref[...]


def rope_tables(pos_col, *, tm):
    T = pos_col.shape[0]
    inv_freq = jnp.exp(-math.log(ROPE_THETA) * jnp.arange(ROPE_HALF, dtype=F32) / ROPE_HALF)
    zeros = jnp.zeros((LANES - ROPE_DIM,), F32)
    freq = jnp.concatenate([inv_freq, inv_freq, zeros]).reshape(1, LANES)
    sign = jnp.concatenate([-jnp.ones((ROPE_HALF,), F32), jnp.ones((ROPE_HALF,), F32),
                            zeros]).reshape(1, LANES)
    row = pl.BlockSpec((1, LANES), lambda i: (0, 0))
    tab = pl.BlockSpec((tm, LANES), lambda i: (i, 0))
    return pl.pallas_call(
        _rope_table_kernel, grid=(T // tm,),
        in_specs=[pl.BlockSpec((tm, 1), lambda i: (i, 0)), row, row],
        out_specs=[tab, tab],
        out_shape=[jax.ShapeDtypeStruct((T, LANES), F32)] * 2,
        compiler_params=_params(1), name="rope_tables",
    )(pos_col, freq, sign)


def _mla_norm_rope(a0, a1, gain, cos, sin, scale):
    ss = jnp.sum(a0 * a0, axis=-1, keepdims=True) + jnp.sum(a1 * a1, axis=-1, keepdims=True)
    r = lax.rsqrt(ss * (1.0 / QK_DIM_B) + EPS)
    y0 = a0 * r * gain[:, :NOPE_DIM]
    y1 = a1 * r * gain[:, NOPE_DIM:]
    lane = lax.broadcasted_iota(jnp.int32, y1.shape, 1)
    partner = jnp.where(lane < ROPE_HALF, pltpu.roll(y1, LANES - ROPE_HALF, 1),
                        pltpu.roll(y1, ROPE_HALF, 1))
    y1 = y1 * cos + partner * sin
    return y0 * scale, y1 * scale


def _latent_kv_kernel(h_ref, ng_ref, wc_ref, wr_ref, cg_ref, c_ref, kr_ref):
    hn = _rms(h_ref[...], ng_ref[...]).astype(BF16)
    c = jnp.dot(hn, wc_ref[...], preferred_element_type=F32)
    c_ref[...] = _rms(c, cg_ref[...]).astype(c_ref.dtype)
    kr_ref[...] = jnp.dot(hn, wr_ref[...], preferred_element_type=F32)


def latent_kv(h, s_norm, wc, wr, c_gain, *, tm):
    T, D = h.shape
    return pl.pallas_call(
        _latent_kv_kernel, grid=(T // tm,),
        in_specs=[
            pl.BlockSpec((tm, D), lambda i: (i, 0)),
            pl.BlockSpec((1, D), lambda i: (0, 0)),
            pl.BlockSpec((D, KV_LORA), lambda i: (0, 0)),
            pl.BlockSpec((D, LANES), lambda i: (0, 0)),
            pl.BlockSpec((1, KV_LORA), lambda i: (0, 0)),
        ],
        out_specs=[pl.BlockSpec((tm, KV_LORA), lambda i: (i, 0)),
                   pl.BlockSpec((tm, LANES), lambda i: (i, 0))],
        out_shape=[jax.ShapeDtypeStruct((T, KV_LORA), BF16),
                   jax.ShapeDtypeStruct((T, LANES), F32)],
        compiler_params=_params(1), name="latent_kv",
    )(h, s_norm.reshape(1, D), wc, wr, c_gain.reshape(1, KV_LORA))


def _kv_up_kernel(c_ref, wk_ref, wv_ref, kr_ref, cos_ref, sin_ref, kg_ref, k_ref, v_ref, *, hpt):
    c = c_ref[...]
    v_ref[...] = jnp.dot(c, wv_ref[...], preferred_element_type=F32).astype(v_ref.dtype)
    kn = jnp.dot(c, wk_ref[...], preferred_element_type=F32)
    kr, cos, sin, gain = kr_ref[...], cos_ref[...], sin_ref[...], kg_ref[...]
    for hh in range(hpt):
        y0, y1 = _mla_norm_rope(kn[:, hh * NOPE_DIM:(hh + 1) * NOPE_DIM], kr, gain, cos, sin, 1.0)
        k_ref[:, hh * MLA_PAD:hh * MLA_PAD + NOPE_DIM] = y0.astype(k_ref.dtype)
        k_ref[:, hh * MLA_PAD + NOPE_DIM:(hh + 1) * MLA_PAD] = y1.astype(k_ref.dtype)


def kv_up(c_kv, wk, wv, k_rope, cos, sin, k_gain_pad, *, tm, hpt):
    T = c_kv.shape[0]
    tab = pl.BlockSpec((tm, LANES), lambda i, j: (i, 0))
    return pl.pallas_call(
        functools.partial(_kv_up_kernel, hpt=hpt),
        grid=(T // tm, N_HEADS // hpt),
        in_specs=[
            pl.BlockSpec((tm, KV_LORA), lambda i, j: (i, 0)),
            pl.BlockSpec((KV_LORA, hpt * NOPE_DIM), lambda i, j: (0, j)),
            pl.BlockSpec((KV_LORA, hpt * HEAD_DIM), lambda i, j: (0, j)),
            tab, tab, tab,
            pl.BlockSpec((1, MLA_PAD), lambda i, j: (0, 0)),
        ],
        out_specs=[pl.BlockSpec((tm, hpt * MLA_PAD), lambda i, j: (i, j)),
                   pl.BlockSpec((tm, hpt * HEAD_DIM), lambda i, j: (i, j))],
        out_shape=[jax.ShapeDtypeStruct((T, N_HEADS * MLA_PAD), BF16),
                   jax.ShapeDtypeStruct((T, N_HEADS * HEAD_DIM), BF16)],
        compiler_params=_params(2), name="kv_up",
    )(c_kv, wk, wv, k_rope, cos, sin, k_gain_pad)


def _q_up_kernel(c_ref, w_ref, cos_ref, sin_ref, qg_ref, q_ref, *, hpt):
    acc = jnp.dot(c_ref[...], w_ref[...], preferred_element_type=F32)
    cos, sin, gain = cos_ref[...], sin_ref[...], qg_ref[...]
    for hh in range(hpt):
        lo = hh * MLA_PAD
        y0, y1 = _mla_norm_rope(acc[:, lo:lo + NOPE_DIM], acc[:, lo + NOPE_DIM:lo + MLA_PAD],
                                gain, cos, sin, QK_DIM_B ** -0.5 * LOG2E)
        q_ref[:, lo:lo + NOPE_DIM] = y0.astype(q_ref.dtype)
        q_ref[:, lo + NOPE_DIM:lo + MLA_PAD] = y1.astype(q_ref.dtype)


def q_up(proj_b, w_uq_pad, cos, sin, q_gain_pad, *, tm, hpt):
    T = proj_b.shape[0]
    tab = pl.BlockSpec((tm, LANES), lambda i, j: (i, 0))
    return pl.pallas_call(
        functools.partial(_q_up_kernel, hpt=hpt),
        grid=(T // tm, N_HEADS // hpt),
        in_specs=[
            pl.BlockSpec((tm, Q_LORA), lambda i, j: (i, 0)),
            pl.BlockSpec((Q_LORA, hpt * MLA_PAD), lambda i, j: (0, j)),
            tab, tab,
            pl.BlockSpec((1, MLA_PAD), lambda i, j: (0, 0)),
        ],
        out_specs=pl.BlockSpec((tm, hpt * MLA_PAD), lambda i, j: (i, j)),
        out_shape=jax.ShapeDtypeStruct((T, N_HEADS * MLA_PAD), BF16),
        compiler_params=_params(2), name="q_up",
    )(proj_b, w_uq_pad, cos, sin, q_gain_pad)


def _pad_cols(w, n):
    return jnp.pad(w, ((0, 0), (0, n - w.shape[1])))


def _pad_heads(w, real, padded):
    K = w.shape[0]
    w = w.reshape(K, N_HEADS, real)
    return jnp.pad(w, ((0, 0), (0, 0), (0, padded - real))).reshape(K, N_HEADS * padded)


def kernel(x, mem, positions, a_norm, a_w_in, a_b_forget, a_q_gain, a_k_gain, a_w_out, b_norm, b_w_in, b_q_latent_norm, b_w_uq, b_q_gain, b_w_out, s_norm, s_w_dkv, s_kv_latent_norm, s_w_ukv, s_k_gain, m_norm, m_w_kv, m_q_gain, m_k_gain, f_norm, d_w_gate, d_w_up, d_w_down, e_router, e_w_gate, e_w_up, e_w_down):
    B, S, D = x.shape
    M = mem.shape[1]
    T = B * S
    depth = f_norm.shape[0]
    n_a = a_w_in.shape[0]
    d_a = N_HEADS * HEAD_DIM
    d_memq = N_HEADS_MEM * HEAD_DIM

    tm = min(512, T)
    tm_big = min(1024, T)
    tq = min(1024, S)
    tk = min(512, S)
    tq_mem = min(1024, S)
    tm_e = min(1024, T)
    tc = min(256, T)
    tf_dense = 512
    tf_moe = 512

    h = x.reshape(T, D)
    mkv = mem_kv(mem.reshape(B * M, D), m_norm, m_w_kv.astype(BF16), m_k_gain)
    q_scale = HEAD_DIM ** -0.5
    cos = sin = shared_k = shared_v = None

    for l in range(depth):
        memq_gain = jnp.tile(m_q_gain[l] * q_scale, N_HEADS_MEM)
        if l < n_a:
            w_main = a_w_in[l, :, :4 * d_a].astype(BF16)
            w_mq, w_f = tail_prep(a_w_in, l, 4 * d_a, N_HEADS, d_memq)
            colgain = jnp.concatenate([
                jnp.tile(a_q_gain[l] * (q_scale * LOG2E), N_HEADS), jnp.tile(a_k_gain[l], N_HEADS),
                jnp.ones((2 * d_a,), F32), memq_gain])
            tn = d_memq
            nt = d_a // tn
            modes = ((0, 2 * nt, HEAD_DIM, False), (2 * nt, 4 * nt, 0, False),
                     (4 * nt, 4 * nt + 1, HEAD_DIM, True))
            proj, f_logit = norm_proj(h, a_norm[l], w_main, colgain, modes, tm=tm_big, tn=tn,
                                      w_tail=w_mq, wf=w_f)
            f_t = f_logit[:, :N_HEADS].reshape(B, S, N_HEADS).transpose(0, 2, 1).reshape(B * N_HEADS, S)
            bias = jnp.tile(a_b_forget[l], B).reshape(B * N_HEADS, 1)
            f_cum = forget_scan(f_t, bias).reshape(B * N_HEADS, 1, S)
            mix = causal_attention(proj, 0, proj, N_HEADS, proj, 2 * N_HEADS, B=B, S=S,
                                   dk=HEAD_DIM, dv=HEAD_DIM, tq=tq, tk=tk,
                                   f_cum=f_cum, g_arr=proj, g_col0=3 * N_HEADS)
            mo = mem_attention(proj, 4 * N_HEADS, mkv, l, B=B, S=S, M=M, tq=tq_mem)
            w_out = a_w_out[l]
        else:
            jb = l - n_a
            if shared_k is None:
                cos, sin = rope_tables(positions.reshape(T, 1).astype(F32), tm=tm)
                c_kv, k_rope = latent_kv(h, s_norm, s_w_dkv[:, :KV_LORA].astype(BF16),
                                         _pad_cols(s_w_dkv[:, KV_LORA:], LANES).astype(BF16),
                                         s_kv_latent_norm, tm=tm)
                w_ukv = s_w_ukv.reshape(KV_LORA, N_HEADS, NOPE_DIM + HEAD_DIM)
                wk = w_ukv[:, :, :NOPE_DIM].reshape(KV_LORA, N_HEADS * NOPE_DIM).astype(BF16)
                wv = w_ukv[:, :, NOPE_DIM:].reshape(KV_LORA, N_HEADS * HEAD_DIM).astype(BF16)
                k_gain_pad = jnp.pad(s_k_gain, (0, MLA_PAD - QK_DIM_B)).reshape(1, MLA_PAD)
                shared_k, shared_v = kv_up(c_kv, wk, wv, k_rope, cos, sin, k_gain_pad, tm=tm_big, hpt=6)
            colgain = jnp.concatenate([b_q_latent_norm[jb], memq_gain])
            modes = ((0, 1, Q_LORA, False), (1, 2, HEAD_DIM, False))
            proj = norm_proj(h, b_norm[jb], b_w_in[jb].astype(BF16), colgain, modes, tm=tm_big, tn=512)
            q_gain_pad = jnp.pad(b_q_gain[jb], (0, MLA_PAD - QK_DIM_B)).reshape(1, MLA_PAD)
            q = q_up(proj, _pad_heads(b_w_uq[jb], QK_DIM_B, MLA_PAD).astype(BF16), cos, sin,
                     q_gain_pad, tm=tm_big, hpt=6)
            mix = causal_attention(q, 0, shared_k, 0, shared_v, 0, B=B, S=S,
                                   dk=MLA_PAD, dv=HEAD_DIM, tq=tq, tk=tk)
            mo = mem_attention(proj, Q_LORA // HEAD_DIM, mkv, l, B=B, S=S, M=M, tq=tq_mem)
            w_out = b_w_out[jb]

        d_mix = mix.shape[1]
        wa, wm = w_out[:d_mix].astype(BF16), w_out[d_mix:].astype(BF16)
        i = l // 2
        if l % 2 == 0:
            h = out_proj(mix, mo, wa, wm, h, tm=tm)
            h = dense_ffn(h, f_norm[l], d_w_gate[i].astype(BF16), d_w_up[i].astype(BF16),
                          d_w_down[i].astype(BF16), tm=tm, tf=tf_dense)
        else:
            h, ids, wts, h_rows = out_proj(mix, mo, wa, wm, h, tm=tm, f_gain=f_norm[l],
                                           w_router=_pad_cols(e_router[i], LANES))
            src, pos1, pos2, te, n_valid, tile_rows = _route_tables(ids, T, tm_e)
            x_sorted = gather_norm_rows(h_rows, src, n_valid, f_norm[l], tm=tm_e)
            y = moe_ffn(x_sorted, te, n_valid, tile_rows, e_w_gate, e_w_up, e_w_down, i,
                        tm=tm_e, tf=tf_moe, sub=min(512, tm_e))
            h = moe_combine(h, wts, y, pos1, pos2, tc=tc)
    return h.reshape(B, S, D)
```

```python
import functools
import math

import jax
import jax.numpy as jnp
from jax import lax
from jax.experimental import pallas as pl
from jax.experimental.pallas import tpu as pltpu

F32 = jnp.float32
BF16 = jnp.bfloat16

EPS = 1e-6
HEAD_DIM = 128
N_HEADS = 12
N_HEADS_MEM = 4
NOPE_DIM = 128
ROPE_DIM = 64
ROPE_HALF = ROPE_DIM // 2
QK_DIM_B = NOPE_DIM + ROPE_DIM
MLA_PAD = 256
Q_LORA = 512
KV_LORA = 512
ROPE_THETA = 10000.0
N_EXPERTS = 8
LANES = 128
LOG2E = math.log2(math.e)
ATTN_HEADS_PER_STEP = 2
VMEM_LIMIT = 56 * 1024 * 1024
MOE_VMEM_LIMIT = 60 * 1024 * 1024


def _params(n_grid, vmem=VMEM_LIMIT):
    return pltpu.CompilerParams(dimension_semantics=("arbitrary",) * n_grid, vmem_limit_bytes=vmem)


def _rms(x, gain):
    ms = jnp.mean(x * x, axis=-1, keepdims=True)
    return x * lax.rsqrt(ms + EPS) * gain


def _norm_proj_kernel(*refs, modes, tn, with_tail):
    if with_tail:
        h_ref, ng_ref, w_ref, cg_ref, wt_ref, wf_ref, o_ref, f_ref, hn_ref = refs
    else:
        h_ref, ng_ref, w_ref, cg_ref, o_ref, hn_ref = refs
        wt_ref = None
    j = pl.program_id(1)

    @pl.when(j == 0)
    def _():
        hn = _rms(h_ref[...], ng_ref[...]).astype(BF16)
        hn_ref[...] = hn
        if with_tail:
            f_ref[...] = jnp.dot(hn, wf_ref[...], preferred_element_type=F32)

    for lo, hi, gs, use_tail in modes:
        @pl.when((j >= lo) & (j < hi))
        def _(gs=gs, use_tail=use_tail):
            w = wt_ref[...] if use_tail else w_ref[...]
            acc = jnp.dot(hn_ref[...], w, preferred_element_type=F32)
            if gs == 0:
                o_ref[...] = acc.astype(o_ref.dtype)
            else:
                for g in range(tn // gs):
                    sl = slice(g * gs, (g + 1) * gs)
                    o_ref[:, sl] = _rms(acc[:, sl], cg_ref[:, sl]).astype(o_ref.dtype)


def norm_proj(h, ngain, w, colgain, modes, *, tm, tn, w_tail=None, wf=None):
    T, D = h.shape
    n_main = w.shape[1] // tn
    with_tail = w_tail is not None
    N = w.shape[1] + (tn if with_tail else 0)
    in_specs = [
        pl.BlockSpec((tm, D), lambda i, j: (i, 0)),
        pl.BlockSpec((1, D), lambda i, j: (0, 0)),
        pl.BlockSpec((D, tn), lambda i, j: (0, jnp.minimum(j, n_main - 1))),
        pl.BlockSpec((1, tn), lambda i, j: (0, j)),
    ]
    args = [h, ngain.reshape(1, D), w, colgain.reshape(1, N)]
    out_shape = [jax.ShapeDtypeStruct((T, N), BF16)]
    out_specs = [pl.BlockSpec((tm, tn), lambda i, j: (i, j))]
    if with_tail:
        in_specs += [pl.BlockSpec((D, tn), lambda i, j: (0, 0)),
                     pl.BlockSpec((D, LANES), lambda i, j: (0, 0))]
        args += [w_tail, wf]
        out_shape.append(jax.ShapeDtypeStruct((T, LANES), F32))
        out_specs.append(pl.BlockSpec((tm, LANES), lambda i, j: (i, 0)))
    outs = pl.pallas_call(
        functools.partial(_norm_proj_kernel, modes=modes, tn=tn, with_tail=with_tail),
        grid=(T // tm, N // tn),
        in_specs=in_specs, out_specs=out_specs, out_shape=out_shape,
        scratch_shapes=[pltpu.VMEM((tm, D), BF16)],
        compiler_params=_params(2), name="norm_proj",
    )(*args)
    return outs if with_tail else outs[0]


def _tail_prep_kernel(*refs, n_blocks, n_f, n_tail):
    blocks = [r[...] for r in refs[:n_blocks]]
    shift_ref, wmq_ref, wf_ref = refs[n_blocks:]
    lane = lax.broadcasted_iota(jnp.int32, blocks[0].shape, 1)
    blocks[-1] = jnp.where(lane < n_tail - (n_blocks - 1) * LANES, blocks[-1], 0.0)
    tail = jnp.concatenate(blocks, axis=1).astype(BF16)
    wmq_ref[...] = jnp.dot(tail, shift_ref[...], preferred_element_type=F32).astype(BF16)
    wf_ref[...] = jnp.where(lane < n_f, blocks[0], 0.0).astype(BF16)


def tail_prep(w_tail, n_f, n_mq):
    D, n_tail = w_tail.shape
    assert n_tail == n_f + n_mq
    n_blocks = pl.cdiv(n_tail, LANES)
    row = lax.broadcasted_iota(jnp.int32, (n_blocks * LANES, n_mq), 0)
    col = lax.broadcasted_iota(jnp.int32, (n_blocks * LANES, n_mq), 1)
    shift = (row == col + n_f).astype(BF16)
    in_specs = [pl.BlockSpec((D, LANES), lambda i, k=k: (0, k)) for k in range(n_blocks)]
    in_specs.append(pl.BlockSpec(shift.shape, lambda i: (0, 0)))
    return pl.pallas_call(
        functools.partial(_tail_prep_kernel, n_blocks=n_blocks, n_f=n_f, n_tail=n_tail),
        grid=(1,),
        in_specs=in_specs,
        out_specs=[pl.BlockSpec((D, n_mq), lambda i: (0, 0)),
                   pl.BlockSpec((D, LANES), lambda i: (0, 0))],
        out_shape=[jax.ShapeDtypeStruct((D, n_mq), BF16), jax.ShapeDtypeStruct((D, LANES), BF16)],
        compiler_params=_params(1), name="tail_prep",
    )(*([w_tail] * n_blocks), shift)


def _forget_scan_kernel(x_ref, b_ref, tri_ref, o_ref, *, n_chunks):
    z = x_ref[...] + b_ref[...]
    log_f = jnp.minimum(z, 0.0) - jnp.log1p(jnp.exp(-jnp.abs(z)))
    tri = tri_ref[...]
    carry = jnp.zeros((z.shape[0], 1), F32)
    for c in range(n_chunks):
        sl = slice(c * LANES, (c + 1) * LANES)
        cs = jnp.dot(log_f[:, sl], tri, precision=lax.Precision.HIGHEST,
                     preferred_element_type=F32) + carry
        o_ref[:, sl] = cs * LOG2E
        carry = cs[:, LANES - 1:LANES]


def forget_scan(f_logit_t, bias_col):
    R, S = f_logit_t.shape
    tri = (lax.broadcasted_iota(jnp.int32, (LANES, LANES), 0)
           <= lax.broadcasted_iota(jnp.int32, (LANES, LANES), 1)).astype(F32)
    return pl.pallas_call(
        functools.partial(_forget_scan_kernel, n_chunks=S // LANES),
        out_shape=jax.ShapeDtypeStruct((R, S), F32), name="forget_scan",
    )(f_logit_t, bias_col, tri)


def _flash_kernel(*refs, tq, tk, dk, dv, hp, has_f, has_g):
    it = iter(refs)
    q_ref, k_ref, v_ref = next(it), next(it), next(it)
    f_ref = next(it) if has_f else None
    g_ref = next(it) if has_g else None
    o_ref, m_ref, l_ref, acc_ref = next(it), next(it), next(it), next(it)
    i = pl.program_id(2)
    m_ref[...] = jnp.full(m_ref.shape, -jnp.inf, F32)
    l_ref[...] = jnp.zeros(l_ref.shape, F32)
    acc_ref[...] = jnp.zeros(acc_ref.shape, F32)
    rep = tk // LANES

    def step(j, r0):
        masked = r0 is not None
        r0 = r0 or 0
        n = tq - r0
        off = pl.multiple_of(j * tk, tk)
        for hh in range(hp):
            s = lax.dot_general(q_ref[r0:, hh * dk:(hh + 1) * dk],
                                k_ref[pl.ds(off, tk), hh * dk:(hh + 1) * dk],
                                (((1,), (1,)), ((), ())), preferred_element_type=F32)
            if has_f:
                s = s - f_ref[hh, :, pl.ds(off, tk)]
            if masked:
                row = i * tq + r0 + lax.broadcasted_iota(jnp.int32, (n, tk), 0)
                col = off + lax.broadcasted_iota(jnp.int32, (n, tk), 1)
                s = jnp.where(col <= row, s, -jnp.inf)
            m_prev = m_ref[hh, r0:, :]
            m_new = jnp.maximum(m_prev, jnp.max(s, axis=-1, keepdims=True))
            alpha = jnp.exp2(m_prev - m_new)
            p = jnp.exp2(s - jnp.concatenate([m_new] * rep, axis=1))
            l_ref[hh, r0:, :] = alpha * l_ref[hh, r0:, :] + jnp.sum(p, axis=-1, keepdims=True)
            acc_ref[hh, r0:, :] = alpha * acc_ref[hh, r0:, :] + jnp.dot(
                p.astype(BF16), v_ref[pl.ds(off, tk), hh * dv:(hh + 1) * dv],
                preferred_element_type=F32)
            m_ref[hh, r0:, :] = m_new

    r = tq // tk

    def full_step(j, c):
        step(j, None)
        return c

    lax.fori_loop(0, i * r, full_step, 0)
    for d in range(r):
        step(i * r + d, d * tk)
    for hh in range(hp):
        o = acc_ref[hh] / l_ref[hh]
        if has_g:
            o = o * jax.nn.sigmoid(g_ref[:, hh * dv:(hh + 1) * dv].astype(F32))
        o_ref[:, hh * dv:(hh + 1) * dv] = o.astype(o_ref.dtype)


def causal_attention(q_arr, q_col0, k_arr, k_col0, v_arr, v_col0, *, B, S, dk, dv, tq, tk,
                     f_cum=None, g_arr=None, g_col0=0):
    hp = ATTN_HEADS_PER_STEP
    assert dv == LANES and tq % tk == 0
    assert q_col0 % hp == 0 and k_col0 % hp == 0 and v_col0 % hp == 0 and g_col0 % hp == 0
    nq = S // tq
    T = B * S
    has_f, has_g = f_cum is not None, g_arr is not None
    in_specs = [
        pl.BlockSpec((tq, hp * dk), lambda b, h, i: (b * nq + i, q_col0 // hp + h)),
        pl.BlockSpec((S, hp * dk), lambda b, h, i: (b, k_col0 // hp + h)),
        pl.BlockSpec((S, hp * dv), lambda b, h, i: (b, v_col0 // hp + h)),
    ]
    args = [q_arr, k_arr, v_arr]
    if has_f:
        in_specs.append(pl.BlockSpec((hp, 1, S), lambda b, h, i: (b * (N_HEADS // hp) + h, 0, 0)))
        args.append(f_cum)
    if has_g:
        in_specs.append(pl.BlockSpec((tq, hp * dv), lambda b, h, i: (b * nq + i, g_col0 // hp + h)))
        args.append(g_arr)
    return pl.pallas_call(
        functools.partial(_flash_kernel, tq=tq, tk=tk, dk=dk, dv=dv, hp=hp, has_f=has_f, has_g=has_g),
        grid=(B, N_HEADS // hp, nq),
        in_specs=in_specs,
        out_specs=pl.BlockSpec((tq, hp * dv), lambda b, h, i: (b * nq + i, h)),
        out_shape=jax.ShapeDtypeStruct((T, N_HEADS * dv), BF16),
        scratch_shapes=[pltpu.VMEM((hp, tq, LANES), F32), pltpu.VMEM((hp, tq, LANES), F32),
                        pltpu.VMEM((hp, tq, dv), F32)],
        compiler_params=_params(3), name="causal_attention",
    )(*args)


def _mem_kv_kernel(mem_ref, ng_ref, w_ref, kg_ref, o_ref, *, n_norm_cols):
    x = _rms(mem_ref[...], ng_ref[...]).astype(BF16)
    acc = jnp.dot(x, w_ref[...], preferred_element_type=F32)
    for g in range(acc.shape[1] // HEAD_DIM):
        sl = slice(g * HEAD_DIM, (g + 1) * HEAD_DIM)
        if g * HEAD_DIM < n_norm_cols:
            o_ref[:, sl] = _rms(acc[:, sl], kg_ref[...]).astype(o_ref.dtype)
        else:
            o_ref[:, sl] = acc[:, sl].astype(o_ref.dtype)


def mem_kv(mem2d, m_norm, m_w_kv, m_k_gain):
    L, D, N = m_w_kv.shape
    R = mem2d.shape[0]
    return pl.pallas_call(
        functools.partial(_mem_kv_kernel, n_norm_cols=N // 2),
        grid=(L,),
        in_specs=[
            pl.BlockSpec((R, D), lambda l: (0, 0)),
            pl.BlockSpec((None, 1, D), lambda l: (l, 0, 0)),
            pl.BlockSpec((None, D, N), lambda l: (l, 0, 0)),
            pl.BlockSpec((None, 1, HEAD_DIM), lambda l: (l, 0, 0)),
        ],
        out_specs=pl.BlockSpec((None, R, N), lambda l: (l, 0, 0)),
        out_shape=jax.ShapeDtypeStruct((L, R, N), BF16),
        compiler_params=_params(1), name="mem_kv",
    )(mem2d, m_norm.reshape(L, 1, D), m_w_kv, m_k_gain.reshape(L, 1, HEAD_DIM))


def _mem_attn_kernel(q_ref, k_ref, v_ref, o_ref):
    s = lax.dot_general(q_ref[...], k_ref[...], (((1,), (1,)), ((), ())),
                        preferred_element_type=F32)
    m = jnp.max(s, axis=-1, keepdims=True)
    p = jnp.exp(s - m)
    l = jnp.sum(p, axis=-1, keepdims=True)
    o = jnp.dot(p.astype(BF16), v_ref[...], preferred_element_type=F32) / l
    o_ref[...] = o.astype(o_ref.dtype)


def mem_attention(q_arr, q_col0, mkv, layer, *, B, S, M, tq):
    nq = S // tq
    T = B * S
    return pl.pallas_call(
        _mem_attn_kernel,
        grid=(B, N_HEADS_MEM, nq),
        in_specs=[
            pl.BlockSpec((tq, HEAD_DIM), lambda b, h, i: (b * nq + i, q_col0 + h)),
            pl.BlockSpec((None, M, HEAD_DIM), lambda b, h, i: (layer, b, h)),
            pl.BlockSpec((None, M, HEAD_DIM), lambda b, h, i: (layer, b, N_HEADS_MEM + h)),
        ],
        out_specs=pl.BlockSpec((tq, HEAD_DIM), lambda b, h, i: (b * nq + i, h)),
        out_shape=jax.ShapeDtypeStruct((T, N_HEADS_MEM * HEAD_DIM), BF16),
        compiler_params=_params(3), name="mem_attention",
    )(q_arr, mkv, mkv)


def _out_proj_kernel(*refs, with_router):
    if with_router:
        (xa_ref, xm_ref, wa_ref, wm_ref, h_ref, fg_ref, wr_ref,
         o_ref, ids_ref, wts_ref, rows_ref) = refs
    else:
        xa_ref, xm_ref, wa_ref, wm_ref, h_ref, o_ref = refs
    acc = jnp.dot(xa_ref[...], wa_ref[...], preferred_element_type=F32)
    acc = acc + jnp.dot(xm_ref[...], wm_ref[...], preferred_element_type=F32)
    h_new = h_ref[...] + acc
    o_ref[...] = h_new
    if with_router:
        rows_ref[:, 0, :] = h_new
        hn = _rms(h_new, fg_ref[...])
        hn_hi = hn.astype(BF16)
        hn_lo = (hn - hn_hi.astype(F32)).astype(BF16)
        wr = wr_ref[...]
        wr_hi = wr.astype(BF16)
        wr_lo = (wr - wr_hi.astype(F32)).astype(BF16)
        logits = (jnp.dot(hn_hi, wr_hi, preferred_element_type=F32)
                  + jnp.dot(hn_hi, wr_lo, preferred_element_type=F32)
                  + jnp.dot(hn_lo, wr_hi, preferred_element_type=F32))
        lane = lax.broadcasted_iota(jnp.int32, logits.shape, 1)
        lane_f = lane.astype(F32)
        logits = jnp.where(lane < N_EXPERTS, logits, -jnp.inf)
        l1 = jnp.max(logits, axis=-1, keepdims=True)
        i1 = jnp.min(jnp.where(logits == l1, lane_f, float(LANES)), axis=-1, keepdims=True)
        rest = jnp.where(lane_f == i1, -jnp.inf, logits)
        l2 = jnp.max(rest, axis=-1, keepdims=True)
        i2 = jnp.min(jnp.where(rest == l2, lane_f, float(LANES)), axis=-1, keepdims=True)
        e = jnp.exp(l2 - l1)
        w1 = 1.0 / (1.0 + e)
        w2 = e / (1.0 + e)
        ids_ref[...] = jnp.where(lane == 0, i1, jnp.where(lane == 1, i2, 0.0))
        wts_ref[...] = jnp.where(lane == 0, w1, jnp.where(lane == 1, w2, 0.0))


def out_proj(xa, xm, wa, wm, h, *, tm, f_gain=None, w_router=None):
    T, D = h.shape
    Ka, Km = xa.shape[1], xm.shape[1]
    with_router = w_router is not None
    in_specs = [
        pl.BlockSpec((tm, Ka), lambda i: (i, 0)),
        pl.BlockSpec((tm, Km), lambda i: (i, 0)),
        pl.BlockSpec((Ka, D), lambda i: (0, 0)),
        pl.BlockSpec((Km, D), lambda i: (0, 0)),
        pl.BlockSpec((tm, D), lambda i: (i, 0)),
    ]
    args = [xa, xm, wa, wm, h]
    out_shape = [jax.ShapeDtypeStruct((T, D), F32)]
    out_specs = [pl.BlockSpec((tm, D), lambda i: (i, 0))]
    if with_router:
        in_specs += [pl.BlockSpec((1, D), lambda i: (0, 0)),
                     pl.BlockSpec((D, LANES), lambda i: (0, 0))]
        args += [f_gain.reshape(1, D), w_router]
        out_shape += [jax.ShapeDtypeStruct((T, LANES), F32)] * 2
        out_specs += [pl.BlockSpec((tm, LANES), lambda i: (i, 0))] * 2
        out_shape.append(jax.ShapeDtypeStruct((T, 1, D), F32))
        out_specs.append(pl.BlockSpec((tm, 1, D), lambda i: (i, 0, 0)))
    outs = pl.pallas_call(
        functools.partial(_out_proj_kernel, with_router=with_router),
        grid=(T // tm,),
        in_specs=in_specs, out_specs=out_specs, out_shape=out_shape,
        compiler_params=_params(1), name="out_proj",
    )(*args)
    return outs if with_router else outs[0]


def _swiglu_step(hn, wg_ref, wu_ref, wd_ref):
    g = jnp.dot(hn, wg_ref[...], preferred_element_type=F32)
    u = jnp.dot(hn, wu_ref[...], preferred_element_type=F32)
    a = (g * jax.nn.sigmoid(g)) * u
    return jnp.dot(a.astype(BF16), wd_ref[...], preferred_element_type=F32)


def _dense_ffn_kernel(h_ref, fg_ref, wg_ref, wu_ref, wd_ref, o_ref, hn_ref, acc_ref):
    f = pl.program_id(1)

    @pl.when(f == 0)
    def _():
        hn_ref[...] = _rms(h_ref[...], fg_ref[...]).astype(BF16)
        acc_ref[...] = jnp.zeros(acc_ref.shape, F32)

    acc_ref[...] += _swiglu_step(hn_ref[...], wg_ref, wu_ref, wd_ref)

    @pl.when(f == pl.num_programs(1) - 1)
    def _():
        o_ref[...] = h_ref[...] + acc_ref[...]


def dense_ffn(h, f_gain, wg, wu, wd, *, tm, tf):
    T, D = h.shape
    FF = wg.shape[1]
    return pl.pallas_call(
        _dense_ffn_kernel,
        grid=(T // tm, FF // tf),
        in_specs=[
            pl.BlockSpec((tm, D), lambda i, f: (i, 0)),
            pl.BlockSpec((1, D), lambda i, f: (0, 0)),
            pl.BlockSpec((D, tf), lambda i, f: (0, f)),
            pl.BlockSpec((D, tf), lambda i, f: (0, f)),
            pl.BlockSpec((tf, D), lambda i, f: (f, 0)),
        ],
        out_specs=pl.BlockSpec((tm, D), lambda i, f: (i, 0)),
        out_shape=jax.ShapeDtypeStruct((T, D), F32),
        scratch_shapes=[pltpu.VMEM((tm, D), BF16), pltpu.VMEM((tm, D), F32)],
        compiler_params=_params(2), name="dense_ffn",
    )(h, f_gain.reshape(1, D), wg, wu, wd)


def _moe_ffn_kernel(te_ref, nv_ref, rows_ref, x_ref, wg_ref, wu_ref, wd_ref, y_ref, *, sub):
    i, f = pl.program_id(0), pl.program_id(1)
    tm = x_ref.shape[0]
    rows = rows_ref[i]

    def accumulate(n_rows):
        x = x_ref[0:n_rows, :]
        g = jnp.dot(x, wg_ref[...].astype(BF16), preferred_element_type=F32)
        u = jnp.dot(x, wu_ref[...].astype(BF16), preferred_element_type=F32)
        a = (g * jax.nn.sigmoid(g)) * u
        d = jnp.dot(a.astype(BF16), wd_ref[...].astype(BF16), preferred_element_type=F32)

        @pl.when(f == 0)
        def _():
            y_ref[0:n_rows, :] = d

        @pl.when(f > 0)
        def _():
            y_ref[0:n_rows, :] += d

    @pl.when(rows > sub)
    def _():
        accumulate(tm)

    @pl.when((rows > 0) & (rows <= sub))
    def _():
        accumulate(sub)

    @pl.when((rows <= sub) & (f == 0))
    def _():
        lo = jnp.where(rows > 0, sub, 0)
        for sb in range(tm // sub):
            @pl.when(sb * sub >= lo)
            def _(sb=sb):
                y_ref[sb * sub:(sb + 1) * sub, :] = jnp.zeros((sub, y_ref.shape[1]), y_ref.dtype)


def moe_ffn(x_sorted, tile_expert, n_valid, tile_rows, wg, wu, wd, layer, *, tm, tf, sub):
    P, D = x_sorted.shape
    FF = wg.shape[3]
    nf = FF // tf

    def row_map(i, f, te, nv, rows):
        return (jnp.minimum(i, nv[0] - 1), 0)

    def f_idx(i, f, nv):
        return jnp.where(i < nv[0], f, nf - 1)

    grid_spec = pltpu.PrefetchScalarGridSpec(
        num_scalar_prefetch=3,
        grid=(P // tm, nf),
        in_specs=[
            pl.BlockSpec((tm, D), row_map),
            pl.BlockSpec((None, None, D, tf),
                         lambda i, f, te, nv, rows: (layer, te[i], 0, f_idx(i, f, nv))),
            pl.BlockSpec((None, None, D, tf),
                         lambda i, f, te, nv, rows: (layer, te[i], 0, f_idx(i, f, nv))),
            pl.BlockSpec((None, None, tf, D),
                         lambda i, f, te, nv, rows: (layer, te[i], f_idx(i, f, nv), 0)),
        ],
        out_specs=pl.BlockSpec((tm, D), lambda i, f, te, nv, rows: (i, 0),
                               pipeline_mode=pl.Buffered(1)),
    )
    return pl.pallas_call(
        functools.partial(_moe_ffn_kernel, sub=sub), grid_spec=grid_spec,
        out_shape=jax.ShapeDtypeStruct((P, D), F32),
        compiler_params=_params(2, MOE_VMEM_LIMIT), name="moe_ffn",
    )(tile_expert, n_valid, tile_rows, x_sorted, wg, wu, wd)


def _gather_norm_kernel(src_ref, nv_ref, h_any, fg_ref, o_ref, buf, sem, *, tm):
    i = pl.program_id(0)
    n_valid = nv_ref[0]

    def start_tile(tile, slot):
        def issue(r, c):
            tok = src_ref[tile * tm + r]
            pltpu.make_async_copy(h_any.at[tok], buf.at[slot, pl.ds(r, 1), :],
                                  sem.at[slot]).start()
            return c

        lax.fori_loop(0, tm, issue, 0)

    @pl.when(i == 0)
    def _():
        start_tile(0, 0)

    @pl.when(i + 1 < n_valid)
    def _():
        start_tile(i + 1, (i + 1) % 2)

    @pl.when(i < n_valid)
    def _():
        slot = i % 2
        pltpu.make_async_copy(h_any.at[pl.ds(0, tm), 0], buf.at[slot], sem.at[slot]).wait()
        o_ref[...] = _rms(buf[slot], fg_ref[...]).astype(o_ref.dtype)

    @pl.when(i >= n_valid)
    def _():
        o_ref[...] = jnp.zeros(o_ref.shape, o_ref.dtype)


def gather_norm_rows(h, src_tok, n_valid, f_gain, *, tm):
    T, _, D = h.shape
    P = src_tok.shape[0]
    grid_spec = pltpu.PrefetchScalarGridSpec(
        num_scalar_prefetch=2,
        grid=(P // tm,),
        in_specs=[pl.BlockSpec(memory_space=pl.ANY),
                  pl.BlockSpec((1, D), lambda i, src, nv: (0, 0))],
        out_specs=pl.BlockSpec((tm, D), lambda i, src, nv: (i, 0)),
        scratch_shapes=[pltpu.VMEM((2, tm, D), F32), pltpu.SemaphoreType.DMA((2,))],
    )
    return pl.pallas_call(
        functools.partial(_gather_norm_kernel, tm=tm), grid_spec=grid_spec,
        out_shape=jax.ShapeDtypeStruct((P, D), BF16),
        compiler_params=_params(1), name="gather_norm_rows",
    )(src_tok, n_valid, h, f_gain.reshape(1, D))


def _combine_kernel(p1_ref, p2_ref, h_ref, w_ref, y_any, o_ref, buf, sem, *, tc):
    i = pl.program_id(0)

    def start_tile(tile, slot):
        def issue(r, c):
            pltpu.make_async_copy(y_any.at[pl.ds(p1_ref[tile * tc + r], 1), :],
                                  buf.at[slot, 0, pl.ds(r, 1), :], sem.at[slot]).start()
            pltpu.make_async_copy(y_any.at[pl.ds(p2_ref[tile * tc + r], 1), :],
                                  buf.at[slot, 1, pl.ds(r, 1), :], sem.at[slot]).start()
            return c

        lax.fori_loop(0, tc, issue, 0)

    @pl.when(i == 0)
    def _():
        start_tile(0, 0)

    @pl.when(i + 1 < pl.num_programs(0))
    def _():
        start_tile(i + 1, (i + 1) % 2)

    slot = i % 2
    for k in range(2):
        pltpu.make_async_copy(y_any.at[pl.ds(0, tc), :], buf.at[slot, k], sem.at[slot]).wait()
    w = w_ref[...]
    o_ref[...] = h_ref[...] + w[:, 0:1] * buf[slot, 0] + w[:, 1:2] * buf[slot, 1]


def moe_combine(h, wts, y, pos1, pos2, *, tc):
    T, D = h.shape
    grid_spec = pltpu.PrefetchScalarGridSpec(
        num_scalar_prefetch=2,
        grid=(T // tc,),
        in_specs=[
            pl.BlockSpec((tc, D), lambda i, p1, p2: (i, 0)),
            pl.BlockSpec((tc, LANES), lambda i, p1, p2: (i, 0)),
            pl.BlockSpec(memory_space=pl.ANY),
        ],
        out_specs=pl.BlockSpec((tc, D), lambda i, p1, p2: (i, 0)),
        scratch_shapes=[pltpu.VMEM((2, 2, tc, D), F32), pltpu.SemaphoreType.DMA((2,))],
    )
    return pl.pallas_call(
        functools.partial(_combine_kernel, tc=tc), grid_spec=grid_spec,
        out_shape=jax.ShapeDtypeStruct((T, D), F32),
        compiler_params=_params(1), name="moe_combine",
    )(pos1, pos2, h, wts, y)


def _route_tables(ids, T, tm):
    e_flat = jnp.concatenate([ids[:, 0], ids[:, 1]]).astype(jnp.int32)
    onehot = (e_flat[:, None] == jnp.arange(N_EXPERTS, dtype=jnp.int32)[None, :]).astype(jnp.int32)
    csum = jnp.cumsum(onehot, axis=0)
    counts = csum[-1]
    rank = jnp.sum((csum - onehot) * onehot, axis=1)
    padded = ((counts + tm - 1) // tm) * tm
    ends = jnp.cumsum(padded)
    starts = ends - padded
    pos = jnp.sum(onehot * starts[None, :], axis=1) + rank
    n_tiles = (2 * T) // tm + N_EXPERTS
    n_valid = ends[-1] // tm
    tile_start = jnp.arange(n_tiles, dtype=jnp.int32) * tm
    te = jnp.sum((tile_start[:, None] >= ends[None, :]).astype(jnp.int32), axis=1)
    last_e = jnp.max(jnp.where(counts > 0, jnp.arange(N_EXPERTS, dtype=jnp.int32), 0))
    te = jnp.minimum(te, last_e)
    tile_rows = jnp.clip(counts[te] - (tile_start - starts[te]), 0, tm)
    tile_rows = jnp.where(tile_start < ends[-1], tile_rows, 0).astype(jnp.int32)
    tok = jnp.arange(T, dtype=jnp.int32)
    src = jnp.zeros((n_tiles * tm,), jnp.int32).at[pos].set(jnp.concatenate([tok, tok]))
    return src, pos[:T], pos[T:], te, n_valid.reshape(1).astype(jnp.int32), tile_rows


def _rope_table_kernel(pos_ref, freq_ref, sign_ref, cos_ref, sin_ref):
    ang = pos_ref[...] * freq_ref[...]
    valid = sign_ref[...] != 0.0
    cos_ref[...] = jnp.where(valid, jnp.cos(ang), 0.0)
    sin_ref[...] = jnp.sin(ang) * sign_ref[...]


def rope_tables(pos_col, *, tm):
    T = pos_col.shape[0]
    inv_freq = jnp.exp(-math.log(ROPE_THETA) * jnp.arange(ROPE_HALF, dtype=F32) / ROPE_HALF)
    zeros = jnp.zeros((LANES - ROPE_DIM,), F32)
    freq = jnp.concatenate([inv_freq, inv_freq, zeros]).reshape(1, LANES)
    sign = jnp.concatenate([-jnp.ones((ROPE_HALF,), F32), jnp.ones((ROPE_HALF,), F32),
                            zeros]).reshape(1, LANES)
    row = pl.BlockSpec((1, LANES), lambda i: (0, 0))
    tab = pl.BlockSpec((tm, LANES), lambda i: (i, 0))
    return pl.pallas_call(
        _rope_table_kernel, grid=(T // tm,),
        in_specs=[pl.BlockSpec((tm, 1), lambda i: (i, 0)), row, row],
        out_specs=[tab, tab],
        out_shape=[jax.ShapeDtypeStruct((T, LANES), F32)] * 2,
        compiler_params=_params(1), name="rope_tables",
    )(pos_col, freq, sign)


def _mla_norm_rope(a0, a1, gain, cos, sin, scale):
    ss = jnp.sum(a0 * a0, axis=-1, keepdims=True) + jnp.sum(a1 * a1, axis=-1, keepdims=True)
    r = lax.rsqrt(ss * (1.0 / QK_DIM_B) + EPS)
    y0 = a0 * r * gain[:, :NOPE_DIM]
    y1 = a1 * r * gain[:, NOPE_DIM:]
    lane = lax.broadcasted_iota(jnp.int32, y1.shape, 1)
    partner = jnp.where(lane < ROPE_HALF, pltpu.roll(y1, LANES - ROPE_HALF, 1),
                        pltpu.roll(y1, ROPE_HALF, 1))
    y1 = y1 * cos + partner * sin
    return y0 * scale, y1 * scale


def _latent_kv_kernel(h_ref, ng_ref, wc_ref, wr_ref, cg_ref, c_ref, kr_ref):
    hn = _rms(h_ref[...], ng_ref[...]).astype(BF16)
    c = jnp.dot(hn, wc_ref[...], preferred_element_type=F32)
    c_ref[...] = _rms(c, cg_ref[...]).astype(c_ref.dtype)
    kr_ref[...] = jnp.dot(hn, wr_ref[...], preferred_element_type=F32)


def latent_kv(h, s_norm, wc, wr, c_gain, *, tm):
    T, D = h.shape
    return pl.pallas_call(
        _latent_kv_kernel, grid=(T // tm,),
        in_specs=[
            pl.BlockSpec((tm, D), lambda i: (i, 0)),
            pl.BlockSpec((1, D), lambda i: (0, 0)),
            pl.BlockSpec((D, KV_LORA), lambda i: (0, 0)),
            pl.BlockSpec((D, LANES), lambda i: (0, 0)),
            pl.BlockSpec((1, KV_LORA), lambda i: (0, 0)),
        ],
        out_specs=[pl.BlockSpec((tm, KV_LORA), lambda i: (i, 0)),
                   pl.BlockSpec((tm, LANES), lambda i: (i, 0))],
        out_shape=[jax.ShapeDtypeStruct((T, KV_LORA), BF16),
                   jax.ShapeDtypeStruct((T, LANES), F32)],
        compiler_params=_params(1), name="latent_kv",
    )(h, s_norm.reshape(1, D), wc, wr, c_gain.reshape(1, KV_LORA))


def _kv_up_kernel(c_ref, wk_ref, wv_ref, kr_ref, cos_ref, sin_ref, kg_ref, k_ref, v_ref, *, hpt):
    c = c_ref[...]
    v_ref[...] = jnp.dot(c, wv_ref[...], preferred_element_type=F32).astype(v_ref.dtype)
    kn = jnp.dot(c, wk_ref[...], preferred_element_type=F32)
    kr, cos, sin, gain = kr_ref[...], cos_ref[...], sin_ref[...], kg_ref[...]
    for hh in range(hpt):
        y0, y1 = _mla_norm_rope(kn[:, hh * NOPE_DIM:(hh + 1) * NOPE_DIM], kr, gain, cos, sin, 1.0)
        k_ref[:, hh * MLA_PAD:hh * MLA_PAD + NOPE_DIM] = y0.astype(k_ref.dtype)
        k_ref[:, hh * MLA_PAD + NOPE_DIM:(hh + 1) * MLA_PAD] = y1.astype(k_ref.dtype)


def kv_up(c_kv, wk, wv, k_rope, cos, sin, k_gain_pad, *, tm, hpt):
    T = c_kv.shape[0]
    tab = pl.BlockSpec((tm, LANES), lambda i, j: (i, 0))
    return pl.pallas_call(
        functools.partial(_kv_up_kernel, hpt=hpt),
        grid=(T // tm, N_HEADS // hpt),
        in_specs=[
            pl.BlockSpec((tm, KV_LORA), lambda i, j: (i, 0)),
            pl.BlockSpec((KV_LORA, hpt * NOPE_DIM), lambda i, j: (0, j)),
            pl.BlockSpec((KV_LORA, hpt * HEAD_DIM), lambda i, j: (0, j)),
            tab, tab, tab,
            pl.BlockSpec((1, MLA_PAD), lambda i, j: (0, 0)),
        ],
        out_specs=[pl.BlockSpec((tm, hpt * MLA_PAD), lambda i, j: (i, j)),
                   pl.BlockSpec((tm, hpt * HEAD_DIM), lambda i, j: (i, j))],
        out_shape=[jax.ShapeDtypeStruct((T, N_HEADS * MLA_PAD), BF16),
                   jax.ShapeDtypeStruct((T, N_HEADS * HEAD_DIM), BF16)],
        compiler_params=_params(2), name="kv_up",
    )(c_kv, wk, wv, k_rope, cos, sin, k_gain_pad)


def _q_up_kernel(c_ref, w_ref, cos_ref, sin_ref, qg_ref, q_ref, *, hpt):
    acc = jnp.dot(c_ref[...], w_ref[...], preferred_element_type=F32)
    cos, sin, gain = cos_ref[...], sin_ref[...], qg_ref[...]
    for hh in range(hpt):
        lo = hh * MLA_PAD
        y0, y1 = _mla_norm_rope(acc[:, lo:lo + NOPE_DIM], acc[:, lo + NOPE_DIM:lo + MLA_PAD],
                                gain, cos, sin, QK_DIM_B ** -0.5 * LOG2E)
        q_ref[:, lo:lo + NOPE_DIM] = y0.astype(q_ref.dtype)
        q_ref[:, lo + NOPE_DIM:lo + MLA_PAD] = y1.astype(q_ref.dtype)


def q_up(proj_b, w_uq_pad, cos, sin, q_gain_pad, *, tm, hpt):
    T = proj_b.shape[0]
    tab = pl.BlockSpec((tm, LANES), lambda i, j: (i, 0))
    return pl.pallas_call(
        functools.partial(_q_up_kernel, hpt=hpt),
        grid=(T // tm, N_HEADS // hpt),
        in_specs=[
            pl.BlockSpec((tm, Q_LORA), lambda i, j: (i, 0)),
            pl.BlockSpec((Q_LORA, hpt * MLA_PAD), lambda i, j: (0, j)),
            tab, tab,
            pl.BlockSpec((1, MLA_PAD), lambda i, j: (0, 0)),
        ],
        out_specs=pl.BlockSpec((tm, hpt * MLA_PAD), lambda i, j: (i, j)),
        out_shape=jax.ShapeDtypeStruct((T, N_HEADS * MLA_PAD), BF16),
        compiler_params=_params(2), name="q_up",
    )(proj_b, w_uq_pad, cos, sin, q_gain_pad)


def _pad_cols(w, n):
    return jnp.pad(w, ((0, 0), (0, n - w.shape[1])))


def _pad_heads(w, real, padded):
    K = w.shape[0]
    w = w.reshape(K, N_HEADS, real)
    return jnp.pad(w, ((0, 0), (0, 0), (0, padded - real))).reshape(K, N_HEADS * padded)


def kernel(x, mem, positions, a_norm, a_w_in, a_b_forget, a_q_gain, a_k_gain, a_w_out, b_norm, b_w_in, b_q_latent_norm, b_w_uq, b_q_gain, b_w_out, s_norm, s_w_dkv, s_kv_latent_norm, s_w_ukv, s_k_gain, m_norm, m_w_kv, m_q_gain, m_k_gain, f_norm, d_w_gate, d_w_up, d_w_down, e_router, e_w_gate, e_w_up, e_w_down):
    B, S, D = x.shape
    M = mem.shape[1]
    T = B * S
    depth = f_norm.shape[0]
    n_a = a_w_in.shape[0]
    d_a = N_HEADS * HEAD_DIM
    d_memq = N_HEADS_MEM * HEAD_DIM

    tm = min(512, T)
    tm_big = min(1024, T)
    tq = min(1024, S)
    tk = min(512, S)
    tq_mem = min(1024, S)
    tm_e = min(1024, T)
    tc = min(256, T)
    tf_dense = 512
    tf_moe = 512

    h = x.reshape(T, D)
    mkv = mem_kv(mem.reshape(B * M, D), m_norm, m_w_kv.astype(BF16), m_k_gain)
    q_scale = HEAD_DIM ** -0.5
    cos = sin = shared_k = shared_v = None

    for l in range(depth):
        memq_gain = jnp.tile(m_q_gain[l] * q_scale, N_HEADS_MEM)
        if l < n_a:
            w_main = a_w_in[l, :, :4 * d_a].astype(BF16)
            w_mq, w_f = tail_prep(a_w_in[l, :, 4 * d_a:], N_HEADS, d_memq)
            colgain = jnp.concatenate([
                jnp.tile(a_q_gain[l] * (q_scale * LOG2E), N_HEADS), jnp.tile(a_k_gain[l], N_HEADS),
                jnp.ones((2 * d_a,), F32), memq_gain])
            tn = d_memq
            nt = d_a // tn
            modes = ((0, 2 * nt, HEAD_DIM, False), (2 * nt, 4 * nt, 0, False),
                     (4 * nt, 4 * nt + 1, HEAD_DIM, True))
            proj, f_logit = norm_proj(h, a_norm[l], w_main, colgain, modes, tm=tm_big, tn=tn,
                                      w_tail=w_mq, wf=w_f)
            f_t = f_logit[:, :N_HEADS].reshape(B, S, N_HEADS).transpose(0, 2, 1).reshape(B * N_HEADS, S)
            bias = jnp.tile(a_b_forget[l], B).reshape(B * N_HEADS, 1)
            f_cum = forget_scan(f_t, bias).reshape(B * N_HEADS, 1, S)
            mix = causal_attention(proj, 0, proj, N_HEADS, proj, 2 * N_HEADS, B=B, S=S,
                                   dk=HEAD_DIM, dv=HEAD_DIM, tq=tq, tk=tk,
                                   f_cum=f_cum, g_arr=proj, g_col0=3 * N_HEADS)
            mo = mem_attention(proj, 4 * N_HEADS, mkv, l, B=B, S=S, M=M, tq=tq_mem)
            w_out = a_w_out[l]
        else:
            jb = l - n_a
            if shared_k is None:
                cos, sin = rope_tables(positions.reshape(T, 1).astype(F32), tm=tm)
                c_kv, k_rope = latent_kv(h, s_norm, s_w_dkv[:, :KV_LORA].astype(BF16),
                                         _pad_cols(s_w_dkv[:, KV_LORA:], LANES).astype(BF16),
                                         s_kv_latent_norm, tm=tm)
                w_ukv = s_w_ukv.reshape(KV_LORA, N_HEADS, NOPE_DIM + HEAD_DIM)
                wk = w_ukv[:, :, :NOPE_DIM].reshape(KV_LORA, N_HEADS * NOPE_DIM).astype(BF16)
                wv = w_ukv[:, :, NOPE_DIM:].reshape(KV_LORA, N_HEADS * HEAD_DIM).astype(BF16)
                k_gain_pad = jnp.pad(s_k_gain, (0, MLA_PAD - QK_DIM_B)).reshape(1, MLA_PAD)
                shared_k, shared_v = kv_up(c_kv, wk, wv, k_rope, cos, sin, k_gain_pad, tm=tm_big, hpt=6)
            colgain = jnp.concatenate([b_q_latent_norm[jb], memq_gain])
            modes = ((0, 1, Q_LORA, False), (1, 2, HEAD_DIM, False))
            proj = norm_proj(h, b_norm[jb], b_w_in[jb].astype(BF16), colgain, modes, tm=tm_big, tn=512)
            q_gain_pad = jnp.pad(b_q_gain[jb], (0, MLA_PAD - QK_DIM_B)).reshape(1, MLA_PAD)
            q = q_up(proj, _pad_heads(b_w_uq[jb], QK_DIM_B, MLA_PAD).astype(BF16), cos, sin,
                     q_gain_pad, tm=tm_big, hpt=6)
            mix = causal_attention(q, 0, shared_k, 0, shared_v, 0, B=B, S=S,
                                   dk=MLA_PAD, dv=HEAD_DIM, tq=tq, tk=tk)
            mo = mem_attention(proj, Q_LORA // HEAD_DIM, mkv, l, B=B, S=S, M=M, tq=tq_mem)
            w_out = b_w_out[jb]

        d_mix = mix.shape[1]
        wa, wm = w_out[:d_mix].astype(BF16), w_out[d_mix:].astype(BF16)
        i = l // 2
        if l % 2 == 0:
            h = out_proj(mix, mo, wa, wm, h, tm=tm)
            h = dense_ffn(h, f_norm[l], d_w_gate[i].astype(BF16), d_w_up[i].astype(BF16),
                          d_w_down[i].astype(BF16), tm=tm, tf=tf_dense)
        else:
            h, ids, wts, h_rows = out_proj(mix, mo, wa, wm, h, tm=tm, f_gain=f_norm[l],
                                           w_router=_pad_cols(e_router[i], LANES))
            src, pos1, pos2, te, n_valid, tile_rows = _route_tables(ids, T, tm_e)
            x_sorted = gather_norm_rows(h_rows, src, n_valid, f_norm[l], tm=tm_e)
            y = moe_ffn(x_sorted, te, n_valid, tile_rows, e_w_gate, e_w_up, e_w_down, i,
                        tm=tm_e, tf=tf_moe, sub=min(512, tm_e))
            h = moe_combine(h, wts, y, pos1, pos2, tc=tc)
    return h.reshape(B, S, D)
```
